```python
import jax, jax.numpy as jnp
from jax import lax
import numpy as np

D_MODEL = 2048
BATCH = 2
SEQ = 16384
DEPTH = 2

HEAD_DIM = 64
D_RWKV = 1024
D_FOX = D_MODEL - D_RWKV
H_RWKV = D_RWKV // HEAD_DIM
H_FOX = D_FOX // HEAD_DIM
D_DECAY_LORA = 64
D_AAA_LORA = 64
D_GATE_LORA = 160
D_FF = 4 * D_MODEL
D_PLE = 256
Q_BLOCK = 128
NORM_EPS = 1e-6
GN_EPS = 64e-5

N_SHIFT = 3 * D_RWKV + D_DECAY_LORA + D_AAA_LORA + D_GATE_LORA
D_IN = N_SHIFT + 3 * D_FOX + H_FOX
RWKV_SPLITS = [D_RWKV, 2 * D_RWKV, 3 * D_RWKV, 3 * D_RWKV + D_DECAY_LORA, 3 * D_RWKV + D_DECAY_LORA + D_AAA_LORA]
FOX_SPLITS = [D_FOX, 2 * D_FOX, 3 * D_FOX]

kernel_name = "hybrid_rwkv7_fox_parallel_heads"


def rmsnorm(x, g):
    xf = x.astype(jnp.float32)
    y = xf * lax.rsqrt(jnp.mean(xf * xf, axis=-1, keepdims=True) + NORM_EPS)
    return (y * g.astype(jnp.float32)).astype(x.dtype)


def rwkv7_mixer(z, w0, w2, a0, a2, g2, k_k, k_a, r_k, gn_w, gn_b):
    B, S, _ = z.shape
    f32 = jnp.float32
    r, k, v, xw, xa, xg = jnp.split(z, RWKV_SPLITS, axis=-1)
    w_log = -jax.nn.softplus(-(w0 + jnp.tanh(xw) @ w2)) - 0.5
    decay = jnp.exp(-jnp.exp(w_log.astype(f32)))
    a = jax.nn.sigmoid(a0 + xa @ a2)
    g = jax.nn.sigmoid(xg) @ g2

    def heads(t):
        return t.reshape(B, S, H_RWKV, HEAD_DIM).astype(f32)

    kk = heads(k * k_k)
    kk = kk / jnp.maximum(jnp.sqrt(jnp.sum(kk * kk, axis=-1, keepdims=True)), 1e-12)
    k = k * (1.0 + (a - 1.0) * k_a)
    r_h, k_h, v_h, w_h, a_h = heads(r), heads(k), heads(v), heads(decay), heads(a)

    def step(state, inp):
        r_t, w_t, k_t, v_t, kk_t, a_t = inp
        sa = jnp.einsum('bhvk,bhk->bhv', state, -kk_t)
        state = (state * w_t[:, :, None, :]
                 + sa[..., None] * (kk_t * a_t)[:, :, None, :]
                 + v_t[..., None] * k_t[:, :, None, :])
        y_t = jnp.einsum('bhvk,bhk->bhv', state, r_t)
        return state, y_t

    xs = tuple(jnp.moveaxis(t, 1, 0) for t in (r_h, w_h, k_h, v_h, kk, a_h))
    state0 = jnp.zeros((B, H_RWKV, HEAD_DIM, HEAD_DIM), f32)
    _, y = lax.scan(step, state0, xs)
    y = jnp.moveaxis(y, 0, 1)
    mu = jnp.mean(y, axis=-1, keepdims=True)
    var = jnp.mean(jnp.square(y - mu), axis=-1, keepdims=True)
    y = ((y - mu) * lax.rsqrt(var + GN_EPS)).reshape(B, S, D_RWKV)
    y = y * gn_w.astype(f32) + gn_b.astype(f32)
    bonus = jnp.sum(r_h * k_h * r_k.astype(f32), axis=-1, keepdims=True) * v_h
    y = y + bonus.reshape(B, S, D_RWKV)
    return (y * g.astype(f32)).astype(z.dtype)


def fox_mixer(q, k, v, f_logit, q_norm, k_norm):
    B, S, _ = q.shape
    nb = S // Q_BLOCK
    f32 = jnp.float32

    def heads(t):
        return t.reshape(B, S, H_FOX, HEAD_DIM).transpose(0, 2, 1, 3)

    qh = rmsnorm(heads(q), q_norm)
    kh = rmsnorm(heads(k), k_norm)
    vh = heads(v)
    log_f = jax.nn.log_sigmoid(f_logit.astype(f32))
    c = jnp.cumsum(log_f, axis=1).transpose(0, 2, 1)
    scale = HEAD_DIM ** -0.5
    qb = qh.reshape(B, H_FOX, nb, Q_BLOCK, HEAD_DIM).transpose(2, 0, 1, 3, 4)
    cb = c.reshape(B, H_FOX, nb, Q_BLOCK).transpose(2, 0, 1, 3)
    pos_q = jnp.arange(S, dtype=jnp.int32).reshape(nb, Q_BLOCK)
    pos_k = jnp.arange(S, dtype=jnp.int32)

    def block(args):
        q_blk, c_blk, pq = args
        logits = jnp.einsum('bhqd,bhkd->bhqk', q_blk, kh).astype(f32) * scale
        logits = logits + (c_blk[..., None] - c[:, :, None, :])
        logits = jnp.where(pq[:, None] >= pos_k[None, :], logits, -jnp.inf)
        probs = jax.nn.softmax(logits, axis=-1)
        return jnp.einsum('bhqk,bhkd->bhqd', probs.astype(vh.dtype), vh)

    o = lax.map(block, (qb, cb, pos_q))
    return o.transpose(1, 0, 3, 2, 4).reshape(B, S, D_FOX)


def setup_inputs(seed: int = 0) -> dict:
    key = jax.random.key(seed)
    ks = jax.random.split(key, 32)
    f32 = jnp.float32

    def nrm(k, shape, scale):
        return jax.random.normal(k, shape, f32) * scale

    L = DEPTH
    return {
        "x": nrm(ks[0], (BATCH, SEQ, D_MODEL), 1.0),
        "p": nrm(ks[1], (DEPTH, BATCH, SEQ, D_PLE), 1.0),
        "norm_mix": 1.0 + nrm(ks[2], (L, D_MODEL), 0.02),
        "w_in": nrm(ks[3], (L, D_MODEL, D_IN), D_MODEL ** -0.5),
        "mu_shift": 0.5 + nrm(ks[4], (L, N_SHIFT), 0.1),
        "w0": nrm(ks[5], (L, D_RWKV), 0.5),
        "w2": nrm(ks[6], (L, D_DECAY_LORA, D_RWKV), D_DECAY_LORA ** -0.5),
        "a0": nrm(ks[7], (L, D_RWKV), 0.1),
        "a2": nrm(ks[8], (L, D_AAA_LORA, D_RWKV), D_AAA_LORA ** -0.5),
        "g2": nrm(ks[9], (L, D_GATE_LORA, D_RWKV), D_GATE_LORA ** -0.5),
        "k_k": 0.85 + nrm(ks[10], (L, D_RWKV), 0.02),
        "k_a": 1.0 + nrm(ks[11], (L, D_RWKV), 0.02),
        "r_k": nrm(ks[12], (L, H_RWKV, HEAD_DIM), 0.1),
        "gn_w": 1.0 + nrm(ks[13], (L, D_RWKV), 0.02),
        "gn_b": nrm(ks[14], (L, D_RWKV), 0.02),
        "b_f": 3.0 + nrm(ks[15], (L, H_FOX), 1.0),
        "q_norm": 1.0 + nrm(ks[16], (L, HEAD_DIM), 0.02),
        "k_norm": 1.0 + nrm(ks[17], (L, HEAD_DIM), 0.02),
        "w_out": nrm(ks[18], (L, D_MODEL, D_MODEL), D_MODEL ** -0.5),
        "norm_mlp": 1.0 + nrm(ks[19], (L, D_MODEL), 0.02),
        "w_ff1": nrm(ks[20], (L, D_MODEL, D_FF), D_MODEL ** -0.5),
        "w_ff2": nrm(ks[21], (L, D_FF, D_MODEL), D_FF ** -0.5),
        "norm_ple": 1.0 + nrm(ks[22], (L, D_MODEL), 0.02),
        "w_ple_gate": nrm(ks[23], (L, D_MODEL, D_MODEL), D_MODEL ** -0.5),
        "w_ple_proj": nrm(ks[24], (L, D_PLE, D_MODEL), D_PLE ** -0.5),
        "norm_final": 1.0 + nrm(ks[25], (D_MODEL,), 0.02),
    }


def reference(x, p, norm_mix, w_in, mu_shift, w0, w2, a0, a2, g2, k_k, k_a, r_k, gn_w, gn_b,
              b_f, q_norm, k_norm, w_out, norm_mlp, w_ff1, w_ff2, norm_ple, w_ple_gate,
              w_ple_proj, norm_final):
    h = x
    for i in range(DEPTH):
        u = rmsnorm(h, norm_mix[i])
        z = u @ w_in[i]
        z_rw, z_fox = z[..., :N_SHIFT], z[..., N_SHIFT:]
        z_prev = jnp.pad(z_rw, ((0, 0), (1, 0), (0, 0)))[:, :-1]
        z_rw = z_rw + (z_prev - z_rw) * mu_shift[i]
        y_rw = rwkv7_mixer(z_rw, w0[i], w2[i], a0[i], a2[i], g2[i], k_k[i], k_a[i], r_k[i],
                           gn_w[i], gn_b[i])
        fq, fk, fv, ff = jnp.split(z_fox, FOX_SPLITS, axis=-1)
        y_fox = fox_mixer(fq, fk, fv, ff + b_f[i], q_norm[i], k_norm[i])
        h = h + jnp.concatenate([y_rw, y_fox], axis=-1) @ w_out[i]
        u = rmsnorm(h, norm_mlp[i])
        h = h + jnp.square(jax.nn.relu(u @ w_ff1[i])) @ w_ff2[i]
        gate = jax.nn.sigmoid(rmsnorm(h, norm_ple[i]) @ w_ple_gate[i])
        h = h + gate * (p[i] @ w_ple_proj[i])
    return rmsnorm(h, norm_final)
```

```python
import functools

import jax
import jax.numpy as jnp
from jax import lax
from jax.experimental import pallas as pl
from jax.experimental.pallas import tpu as pltpu

F32 = jnp.float32
BF16 = jnp.bfloat16

HEAD_DIM = 64
D_RWKV = 1024
D_FOX = 1024
N_LORA = 288
N_LORA_PAD = 384
N_FF_PAD = 128
NORM_EPS = 1e-6
GN_EPS = 64e-5
CHUNK = 64
LANES = 128
VMEM_LIMIT = 56 * 1024 * 1024


def _dot(a, b):
    return jnp.dot(a, b, preferred_element_type=F32)


def _dot_nt(a, b):
    return lax.dot_general(a, b, (((1,), (1,)), ((), ())), preferred_element_type=F32)


def _dot_tn(a, b):
    return lax.dot_general(a, b, (((0,), (0,)), ((), ())), preferred_element_type=F32)


def _split3(x):
    h1 = x.astype(BF16)
    r1 = x - h1.astype(F32)
    h2 = r1.astype(BF16)
    h3 = (r1 - h2.astype(F32)).astype(BF16)
    return h1, h2, h3


def _split2(x):
    h1 = x.astype(BF16)
    h2 = (x - h1.astype(F32)).astype(BF16)
    return h1, h2


def _group_sum(x, bd):
    w = bd.shape[0]
    hi, lo = _split2(x)
    outs = []
    for c in range(x.shape[1] // w):
        sl = slice(c * w, (c + 1) * w)
        outs.append(_dot(hi[:, sl], bd) + _dot(lo[:, sl], bd))
    return outs[0] if len(outs) == 1 else jnp.concatenate(outs, axis=1)


def _cumsum_rows(tri, x):
    h1, h2, h3 = _split3(x)
    return _dot(tri, h1) + _dot(tri, h2) + _dot(tri, h3)


def _sigmoid(x):
    return 1.0 / (1.0 + jnp.exp(-x))


def _softplus(x):
    return jnp.maximum(x, 0.0) + jnp.log1p(jnp.exp(-jnp.abs(x)))


def _rms(x, g):
    ms = jnp.mean(x * x, axis=-1, keepdims=True)
    return x * lax.rsqrt(ms + NORM_EPS) * g


def _params(sem):
    return pltpu.CompilerParams(dimension_semantics=sem, vmem_limit_bytes=VMEM_LIMIT)


def _in_proj_kernel(x_ref, g_ref, w_ref, o_ref, xn_ref):
    @pl.when(pl.program_id(1) == 0)
    def _():
        xn_ref[...] = _rms(x_ref[...], g_ref[...]).astype(BF16)

    o_ref[...] = _dot(xn_ref[...], w_ref[...])


def _in_proj(h, g, w, *, tm, tn):
    t, d = h.shape
    n = w.shape[1]
    return pl.pallas_call(
        _in_proj_kernel,
        out_shape=jax.ShapeDtypeStruct((t, n), F32),
        grid=(t // tm, n // tn),
        in_specs=[
            pl.BlockSpec((tm, d), lambda i, j: (i, 0)),
            pl.BlockSpec((1, d), lambda i, j: (0, 0)),
            pl.BlockSpec((d, tn), lambda i, j: (0, j)),
        ],
        out_specs=pl.BlockSpec((tm, tn), lambda i, j: (i, j)),
        scratch_shapes=[pltpu.VMEM((tm, d), BF16)],
        compiler_params=_params(("parallel", "arbitrary")),
        name="in_proj",
    )(h, g, w)


def _rwkv_prep_kernel(zr_ref, zl_ref, pr_ref, pl_ref, mur_ref, mul_ref, w0_ref, w2_ref, a0_ref,
                      a2_ref, g2_ref, kkw_ref, kaw_ref, bd_ref,
                      r_out, ld_out, kp_out, v_out, kk_out, b_out, g_out, *, blocks_per_seq):
    first = (pl.program_id(0) % blocks_per_seq) == 0

    def shift(z, prev8, mu):
        last = jnp.where(first, 0.0, prev8[7:8, :])
        rolled = pltpu.roll(z, 1, 0)
        row = lax.broadcasted_iota(jnp.int32, z.shape, 0)
        prev = jnp.where(row == 0, last, rolled)
        return z + (prev - z) * mu

    zs = shift(zr_ref[...], pr_ref[...], mur_ref[...])
    zl = shift(zl_ref[...], pl_ref[...], mul_ref[...])
    r = zs[:, :D_RWKV]
    k = zs[:, D_RWKV:2 * D_RWKV]
    v = zs[:, 2 * D_RWKV:]

    lw = w0_ref[...] + _dot(jnp.tanh(zl).astype(BF16), w2_ref[...])
    w_log = -_softplus(-lw) - 0.5
    ld = -jnp.exp(w_log)
    a = _sigmoid(a0_ref[...] + _dot(zl.astype(BF16), a2_ref[...]))
    g = _dot(_sigmoid(zl).astype(BF16), g2_ref[...])

    kkr = k * kkw_ref[...]
    ss = _group_sum(kkr * kkr, bd_ref[...])
    kk = kkr / jnp.maximum(jnp.sqrt(ss), 1e-12)
    kp = k * (1.0 + (a - 1.0) * kaw_ref[...])

    r_out[...] = r
    ld_out[...] = ld
    kp_out[...] = kp
    v_out[...] = v
    kk_out[...] = kk
    b_out[...] = kk * a
    g_out[...] = g


def _rwkv_prep(z, seq, mu_r, mu_l, w0, w2p, a0, a2p, g2p, k_k, k_a, bd, *, tm):
    t = z.shape[0]
    c_lora = (3 * D_RWKV + 3 * D_FOX) // N_LORA_PAD
    rows8 = tm // 8
    row = lambda i: (0, 0)
    prev = lambda i: (jnp.maximum(i * rows8 - 1, 0), 0)
    out = jax.ShapeDtypeStruct((t, D_RWKV), F32)
    return pl.pallas_call(
        functools.partial(_rwkv_prep_kernel, blocks_per_seq=seq // tm),
        out_shape=[out] * 7,
        grid=(t // tm,),
        in_specs=[
            pl.BlockSpec((tm, 3 * D_RWKV), lambda i: (i, 0)),
            pl.BlockSpec((tm, N_LORA_PAD), lambda i: (i, c_lora)),
            pl.BlockSpec((8, 3 * D_RWKV), prev),
            pl.BlockSpec((8, N_LORA_PAD), lambda i: (jnp.maximum(i * rows8 - 1, 0), c_lora)),
            pl.BlockSpec((1, 3 * D_RWKV), row),
            pl.BlockSpec((1, N_LORA_PAD), row),
            pl.BlockSpec((1, D_RWKV), row),
            pl.BlockSpec((N_LORA_PAD, D_RWKV), row),
            pl.BlockSpec((1, D_RWKV), row),
            pl.BlockSpec((N_LORA_PAD, D_RWKV), row),
            pl.BlockSpec((N_LORA_PAD, D_RWKV), row),
            pl.BlockSpec((1, D_RWKV), row),
            pl.BlockSpec((1, D_RWKV), row),
            pl.BlockSpec(bd.shape, row),
        ],
        out_specs=[pl.BlockSpec((tm, D_RWKV), lambda i: (i, 0))] * 7,
        compiler_params=_params(("parallel",)),
        name="rwkv_prep",
    )(z, z, z, z, mu_r, mu_l, w0, w2p, a0, a2p, g2p, k_k, k_a, bd)


def _stack2(q, m0):
    zero = jnp.zeros_like(q)
    return jnp.concatenate([jnp.where(m0, q, zero), jnp.where(m0, zero, q)], axis=0)


def _scan_kernel(r_ref, ld_ref, kp_ref, v_ref, kk_ref, b_ref, g_ref, gnw_ref, gnb_ref, rk_ref,
                 bd_ref, y_ref, s_ref, *, nchunk):
    L = CHUNK

    @pl.when(pl.program_id(2) == 0)
    def _():
        s_ref[...] = jnp.zeros_like(s_ref)

    lane = lax.broadcasted_iota(jnp.int32, (1, LANES), 1)
    m0 = lane < HEAD_DIM
    row = lax.broadcasted_iota(jnp.int32, (L, LANES), 0)
    col = lax.broadcasted_iota(jnp.int32, (L, LANES), 1) % HEAD_DIM
    strict = row > col
    incl = row >= col
    eye = (row == col).astype(F32)
    tri = (lax.broadcasted_iota(jnp.int32, (L, L), 0)
           >= lax.broadcasted_iota(jnp.int32, (L, L), 1)).astype(BF16)
    bd = bd_ref[...]

    def pairmm(p, q):
        return _dot(p.astype(BF16), _stack2(q, m0).astype(BF16))

    s = s_ref[...]
    for c in range(nchunk):
        sl = slice(c * L, (c + 1) * L)
        r = r_ref[sl, :]
        ld = ld_ref[sl, :]
        kp = kp_ref[sl, :]
        v = v_ref[sl, :]
        kk = kk_ref[sl, :]
        b = b_ref[sl, :]

        cum = _cumsum_rows(tri, ld)
        w_inv = jnp.exp(-cum)
        w_last = jnp.exp(cum[L - 1:L, :])
        abar = -kk * jnp.exp(cum - ld)
        rbar = r * jnp.exp(cum)
        btil = b * w_inv
        ktil = kp * w_inv
        bhat = btil * w_last
        khat = ktil * w_last

        x = jnp.concatenate([abar, rbar], axis=0).astype(BF16)
        yst = jnp.concatenate([_stack2(btil, m0), _stack2(ktil, m0)], axis=0).astype(BF16)
        a_all = _dot_nt(x, yst)
        zero = jnp.zeros((L, LANES), F32)
        n = jnp.where(strict, a_all[:L, :LANES], zero)
        a_ak = jnp.where(strict, a_all[:L, LANES:], zero)
        a_rb = jnp.where(incl, a_all[L:, :LANES], zero)
        a_rk = jnp.where(incl, a_all[L:, LANES:], zero)

        tinv = eye + n
        nk = n
        for _ in range(5):
            nk = pairmm(nk, nk)
            tinv = tinv + pairmm(tinv, nk)

        xs = _dot_nt(x, _stack2(s, m0).astype(BF16))
        av = _dot(jnp.concatenate([a_ak, a_rk], axis=0).astype(BF16),
                  _stack2(v, m0).astype(BF16))
        u = pairmm(tinv, xs[:L] + av[:L])
        y = xs[L:] + pairmm(a_rb, u) + av[L:]
        full = _dot_tn(jnp.concatenate([u, v], axis=0).astype(BF16),
                       jnp.concatenate([bhat, khat], axis=0).astype(BF16))
        s = s * w_last + jnp.where(m0, full[:HEAD_DIM], full[HEAD_DIM:])

        mu = _group_sum(y, bd) * (1.0 / HEAD_DIM)
        d = y - mu
        var = _group_sum(d * d, bd) * (1.0 / HEAD_DIM)
        yn = d * lax.rsqrt(var + GN_EPS) * gnw_ref[...] + gnb_ref[...]
        bonus = _group_sum(r * kp * rk_ref[...], bd) * v
        y_ref[sl, :] = ((yn + bonus) * g_ref[sl, :]).astype(y_ref.dtype)

    s_ref[...] = s


def _rwkv_scan(feats, g, gn_w, gn_b, r_k, bd128, batch, seq, *, tc):
    npair = D_RWKV // LANES
    tok = pl.BlockSpec((None, tc, LANES), lambda b, p, c: (b, c, p))
    par = pl.BlockSpec((1, LANES), lambda b, p, c: (0, p))
    feats = [f.reshape(batch, seq, D_RWKV) for f in feats]
    return pl.pallas_call(
        functools.partial(_scan_kernel, nchunk=tc // CHUNK),
        out_shape=jax.ShapeDtypeStruct((batch, seq, D_RWKV), BF16),
        grid=(batch, npair, seq // tc),
        in_specs=[tok] * 7 + [par] * 3 + [pl.BlockSpec((LANES, LANES), lambda b, p, c: (0, 0))],
        out_specs=tok,
        scratch_shapes=[pltpu.VMEM((HEAD_DIM, LANES), F32)],
        compiler_params=_params(("parallel", "parallel", "arbitrary")),
        name="rwkv_scan",
    )(*feats, g.reshape(batch, seq, D_RWKV), gn_w, gn_b, r_k, bd128)


def _fox_prep_kernel(q_ref, k_ref, v_ref, ff_ref, gq_ref, gk_ref, bf_ref, bd_ref,
                     qo_ref, ko_ref, vo_ref, c_ref, ct_ref, carry_ref, *, blocks_per_seq):
    @pl.when((pl.program_id(0) % blocks_per_seq) == 0)
    def _():
        carry_ref[...] = jnp.zeros_like(carry_ref)

    bd = bd_ref[...]
    q = q_ref[...]
    k = k_ref[...]
    msq = _group_sum(q * q, bd) * (1.0 / HEAD_DIM)
    msk = _group_sum(k * k, bd) * (1.0 / HEAD_DIM)
    qo_ref[...] = (q * lax.rsqrt(msq + NORM_EPS) * gq_ref[...] * (HEAD_DIM ** -0.5)).astype(BF16)
    ko_ref[...] = (k * lax.rsqrt(msk + NORM_EPS) * gk_ref[...]).astype(BF16)
    vo_ref[...] = v_ref[...].astype(BF16)

    tm = q.shape[0]
    log_f = -_softplus(-(ff_ref[...] + bf_ref[...]))
    tri = (lax.broadcasted_iota(jnp.int32, (tm, tm), 0)
           >= lax.broadcasted_iota(jnp.int32, (tm, tm), 1)).astype(BF16)
    c = _cumsum_rows(tri, log_f) + carry_ref[...]
    carry_ref[...] = c[tm - 1:tm, :]
    c_ref[...] = c
    ct_ref[...] = c.T[:ct_ref.shape[0], :]


def _fox_prep(z, seq, gq, gk, bf, bd, *, tm):
    t = z.shape[0]
    nh = D_FOX // HEAD_DIM
    c_ff = (3 * D_RWKV + 3 * D_FOX + N_LORA_PAD) // N_FF_PAD
    row = lambda i: (0, 0)
    qkv = jax.ShapeDtypeStruct((t, D_FOX), BF16)
    return pl.pallas_call(
        functools.partial(_fox_prep_kernel, blocks_per_seq=seq // tm),
        out_shape=[qkv, qkv, qkv, jax.ShapeDtypeStruct((t, N_FF_PAD), F32),
                   jax.ShapeDtypeStruct((nh, t), F32)],
        grid=(t // tm,),
        in_specs=[
            pl.BlockSpec((tm, D_FOX), lambda i: (i, 3)),
            pl.BlockSpec((tm, D_FOX), lambda i: (i, 4)),
            pl.BlockSpec((tm, D_FOX), lambda i: (i, 5)),
            pl.BlockSpec((tm, N_FF_PAD), lambda i: (i, c_ff)),
            pl.BlockSpec((1, D_FOX), row),
            pl.BlockSpec((1, D_FOX), row),
            pl.BlockSpec((1, N_FF_PAD), row),
            pl.BlockSpec(bd.shape, row),
        ],
        out_specs=[pl.BlockSpec((tm, D_FOX), lambda i: (i, 0))] * 3
        + [pl.BlockSpec((tm, N_FF_PAD), lambda i: (i, 0)), pl.BlockSpec((nh, tm), lambda i: (0, i))],
        scratch_shapes=[pltpu.VMEM((1, N_FF_PAD), F32)],
        compiler_params=_params(("arbitrary",)),
        name="fox_prep",
    )(z, z, z, z, gq, gk, bf, bd)


def _attn_kernel(q_ref, k_ref, v_ref, cc_ref, ct_ref, o_ref, *, tq):
    p = pl.program_id(1)
    i = pl.program_id(2)
    lane = lax.broadcasted_iota(jnp.int32, (1, LANES), 1)
    m0 = lane < HEAD_DIM
    q = q_ref[...]
    zq = jnp.zeros_like(q)
    qs = (jnp.where(m0, q, zq), jnp.where(m0, zq, q))
    cc = cc_ref[...]
    lane_q = lax.broadcasted_iota(jnp.int32, cc.shape, 1)
    ci = tuple(jnp.sum(jnp.where(lane_q == 2 * p + e, cc, 0.0), axis=1, keepdims=True)
               for e in range(2))

    def step(j, carry, masked):
        off = pl.multiple_of(j * tq, tq)
        kj = k_ref[pl.ds(off, tq), :]
        vj = v_ref[pl.ds(off, tq), :]
        new = []
        for e in range(2):
            m, l, acc = carry[e]
            cj = ct_ref[pl.ds(2 * p + e, 1), pl.ds(off, tq)]
            s = _dot_nt(qs[e], kj) + (ci[e] - cj)
            if masked:
                rr = lax.broadcasted_iota(jnp.int32, s.shape, 0)
                cl = lax.broadcasted_iota(jnp.int32, s.shape, 1)
                s = jnp.where(rr >= cl, s, -jnp.inf)
            m_new = jnp.maximum(m, jnp.max(s, axis=1, keepdims=True))
            alpha = jnp.exp(m - m_new)
            pr = jnp.exp(s - m_new)
            l = alpha * l + jnp.sum(pr, axis=1, keepdims=True)
            acc = alpha * acc + _dot(pr.astype(BF16), vj)
            new.append((m_new, l, acc))
        return tuple(new)

    init = tuple((jnp.full((tq, 1), -jnp.inf, F32), jnp.zeros((tq, 1), F32),
                  jnp.zeros((tq, LANES), F32)) for _ in range(2))
    carry = lax.fori_loop(0, i, lambda j, c: step(j, c, False), init)
    carry = step(i, carry, True)
    (_, l0, acc0), (_, l1, acc1) = carry
    o_ref[...] = jnp.where(m0, acc0 / l0, acc1 / l1).astype(o_ref.dtype)


def _fox_attention(q, k, v, c, ct, batch, seq, *, tq):
    npair = D_FOX // LANES
    nh = D_FOX // HEAD_DIM
    q = q.reshape(batch, seq, D_FOX)
    k = k.reshape(batch, seq, D_FOX)
    v = v.reshape(batch, seq, D_FOX)
    c = c.reshape(batch, seq, N_FF_PAD)
    kv = pl.BlockSpec((None, seq, LANES), lambda b, p, i: (b, 0, p))
    return pl.pallas_call(
        functools.partial(_attn_kernel, tq=tq),
        out_shape=jax.ShapeDtypeStruct((batch, seq, D_FOX), BF16),
        grid=(batch, npair, seq // tq),
        in_specs=[
            pl.BlockSpec((None, tq, LANES), lambda b, p, i: (b, i, p)),
            kv, kv,
            pl.BlockSpec((None, tq, N_FF_PAD), lambda b, p, i: (b, i, 0)),
            pl.BlockSpec((nh, seq), lambda b, p, i: (0, b)),
        ],
        out_specs=pl.BlockSpec((None, tq, LANES), lambda b, p, i: (b, i, p)),
        compiler_params=_params(("parallel", "parallel", "arbitrary")),
        name="fox_attn",
    )(q, k, v, c, ct)


def _out_proj_kernel(h_ref, y1_ref, y2_ref, w1_ref, w2_ref, o_ref):
    o_ref[...] = h_ref[...] + _dot(y1_ref[...], w1_ref[...]) + _dot(y2_ref[...], w2_ref[...])


def _out_proj(h, y1, y2, w1, w2, *, tm, tn):
    t, d = h.shape
    return pl.pallas_call(
        _out_proj_kernel,
        out_shape=jax.ShapeDtypeStruct((t, d), F32),
        grid=(t // tm, d // tn),
        in_specs=[
            pl.BlockSpec((tm, tn), lambda i, j: (i, j)),
            pl.BlockSpec((tm, y1.shape[1]), lambda i, j: (i, 0)),
            pl.BlockSpec((tm, y2.shape[1]), lambda i, j: (i, 0)),
            pl.BlockSpec((w1.shape[0], tn), lambda i, j: (0, j)),
            pl.BlockSpec((w2.shape[0], tn), lambda i, j: (0, j)),
        ],
        out_specs=pl.BlockSpec((tm, tn), lambda i, j: (i, j)),
        compiler_params=_params(("parallel", "parallel")),
        name="out_proj",
    )(h, y1, y2, w1, w2)


def _mlp_kernel(x_ref, g_ref, w1_ref, w2_ref, o_ref, xn_ref):
    @pl.when(pl.program_id(1) == 0)
    def _():
        x = x_ref[...]
        xn_ref[...] = _rms(x, g_ref[...]).astype(BF16)
        o_ref[...] = x

    hid = jnp.maximum(_dot(xn_ref[...], w1_ref[...]), 0.0)
    o_ref[...] += _dot((hid * hid).astype(BF16), w2_ref[...])


def _mlp(h, g, w1, w2, *, tm, tf):
    t, d = h.shape
    f = w1.shape[1]
    return pl.pallas_call(
        _mlp_kernel,
        out_shape=jax.ShapeDtypeStruct((t, d), F32),
        grid=(t // tm, f // tf),
        in_specs=[
            pl.BlockSpec((tm, d), lambda i, j: (i, 0)),
            pl.BlockSpec((1, d), lambda i, j: (0, 0)),
            pl.BlockSpec((d, tf), lambda i, j: (0, j)),
            pl.BlockSpec((tf, d), lambda i, j: (j, 0)),
        ],
        out_specs=pl.BlockSpec((tm, d), lambda i, j: (i, 0)),
        scratch_shapes=[pltpu.VMEM((tm, d), BF16)],
        compiler_params=_params(("parallel", "arbitrary")),
        name="mlp",
    )(h, g, w1, w2)


def _ple_kernel(x_ref, xc_ref, g_ref, p_ref, wg_ref, wp_ref, o_ref, xn_ref):
    @pl.when(pl.program_id(1) == 0)
    def _():
        xn_ref[...] = _rms(x_ref[...], g_ref[...]).astype(BF16)

    gate = _sigmoid(_dot(xn_ref[...], wg_ref[...]))
    o_ref[...] = xc_ref[...] + gate * _dot(p_ref[...].astype(BF16), wp_ref[...])


def _ple(h, g, p, wg, wp, *, tm, tn):
    t, d = h.shape
    dp = p.shape[1]
    return pl.pallas_call(
        _ple_kernel,
        out_shape=jax.ShapeDtypeStruct((t, d), F32),
        grid=(t // tm, d // tn),
        in_specs=[
            pl.BlockSpec((tm, d), lambda i, j: (i, 0)),
            pl.BlockSpec((tm, tn), lambda i, j: (i, j)),
            pl.BlockSpec((1, d), lambda i, j: (0, 0)),
            pl.BlockSpec((tm, dp), lambda i, j: (i, 0)),
            pl.BlockSpec((d, tn), lambda i, j: (0, j)),
            pl.BlockSpec((dp, tn), lambda i, j: (0, j)),
        ],
        out_specs=pl.BlockSpec((tm, tn), lambda i, j: (i, j)),
        scratch_shapes=[pltpu.VMEM((tm, d), BF16)],
        compiler_params=_params(("parallel", "arbitrary")),
        name="ple",
    )(h, h, g, p, wg, wp)


def _final_norm_kernel(x_ref, g_ref, o_ref):
    o_ref[...] = _rms(x_ref[...], g_ref[...])


def _final_norm(h, g, *, tm):
    t, d = h.shape
    return pl.pallas_call(
        _final_norm_kernel,
        out_shape=jax.ShapeDtypeStruct((t, d), F32),
        grid=(t // tm,),
        in_specs=[pl.BlockSpec((tm, d), lambda i: (i, 0)), pl.BlockSpec((1, d), lambda i: (0, 0))],
        out_specs=pl.BlockSpec((tm, d), lambda i: (i, 0)),
        compiler_params=_params(("parallel",)),
        name="final_norm",
    )(h, g)


def _block_diag_ones(n):
    idx = jnp.arange(n) // HEAD_DIM
    return (idx[:, None] == idx[None, :]).astype(BF16)


def _pad_cols(w, n):
    return jnp.pad(w, ((0, 0), (0, n - w.shape[1])))


def _pack_w_in(w):
    n_shift = 3 * D_RWKV + N_LORA
    rkv = w[:, :3 * D_RWKV]
    lora = _pad_cols(w[:, 3 * D_RWKV:n_shift], N_LORA_PAD)
    fox = w[:, n_shift:n_shift + 3 * D_FOX]
    ff = _pad_cols(w[:, n_shift + 3 * D_FOX:], N_FF_PAD)
    return jnp.concatenate([rkv, fox, lora, ff], axis=1).astype(BF16)


def _lora_rows(w, start):
    return jnp.pad(w, ((start, N_LORA_PAD - start - w.shape[0]), (0, 0))).astype(BF16)


def kernel(x, p, norm_mix, w_in, mu_shift, w0, w2, a0, a2, g2, k_k, k_a, r_k, gn_w, gn_b, b_f, q_norm, k_norm, w_out, norm_mlp, w_ff1, w_ff2, norm_ple, w_ple_gate, w_ple_proj, norm_final):
    batch, seq, d = x.shape
    depth = w_in.shape[0]
    t = batch * seq
    nh_fox = D_FOX // HEAD_DIM
    tm = min(512, seq)
    tq = min(256, seq)
    bd256 = _block_diag_ones(256)
    bd128 = _block_diag_ones(LANES)
    row = lambda a: a.reshape(1, -1)

    h = x.reshape(t, d)
    for i in range(depth):
        z = _in_proj(h, row(norm_mix[i]), _pack_w_in(w_in[i]), tm=tm, tn=512)

        mu = mu_shift[i]
        feats = _rwkv_prep(
            z, seq, row(mu[:3 * D_RWKV]), row(jnp.pad(mu[3 * D_RWKV:], (0, N_LORA_PAD - N_LORA))),
            row(w0[i]), _lora_rows(w2[i], 0), row(a0[i]), _lora_rows(a2[i], 64),
            _lora_rows(g2[i], 128), row(k_k[i]), row(k_a[i]), bd256, tm=min(256, seq))
        y_rw = _rwkv_scan(feats[:6], feats[6], row(gn_w[i]), row(gn_b[i]), row(r_k[i]), bd128,
                          batch, seq, tc=min(256, seq))

        qn, kn, vn, c, ct = _fox_prep(
            z, seq, row(jnp.tile(q_norm[i], nh_fox)), row(jnp.tile(k_norm[i], nh_fox)),
            row(jnp.pad(b_f[i], (0, N_FF_PAD - nh_fox))), bd256, tm=min(256, seq))
        y_fox = _fox_attention(qn, kn, vn, c, ct, batch, seq, tq=tq)

        wo = w_out[i].astype(BF16)
        h = _out_proj(h, y_rw.reshape(t, D_RWKV), y_fox.reshape(t, D_FOX), wo[:D_RWKV], wo[D_RWKV:],
                      tm=tm, tn=512)
        h = _mlp(h, row(norm_mlp[i]), w_ff1[i].astype(BF16), w_ff2[i].astype(BF16), tm=tm, tf=512)
        h = _ple(h, row(norm_ple[i]), p[i].reshape(t, -1), w_ple_gate[i].astype(BF16),
                 w_ple_proj[i].astype(BF16), tm=tm, tn=512)
    return _final_norm(h, row(norm_final), tm=tm).reshape(batch, seq, d)
```

```python
import functools

import jax
import jax.numpy as jnp
from jax import lax
from jax.experimental import pallas as pl
from jax.experimental.pallas import tpu as pltpu

F32 = jnp.float32
BF16 = jnp.bfloat16

HEAD_DIM = 64
D_RWKV = 1024
D_FOX = 1024
N_LORA = 288
N_LORA_PAD = 384
N_FF_PAD = 128
NORM_EPS = 1e-6
GN_EPS = 64e-5
CHUNK = 64
LANES = 128
TK = 256
AUG = 8
LOG2E = 1.4426950408889634
EXP2_UNDERFLOW = 152.0
VMEM_LIMIT = 56 * 1024 * 1024


def _dot(a, b):
    return jnp.dot(a, b, preferred_element_type=F32)


def _dot_nt(a, b):
    return lax.dot_general(a, b, (((1,), (1,)), ((), ())), preferred_element_type=F32)


def _dot_tn(a, b):
    return lax.dot_general(a, b, (((0,), (0,)), ((), ())), preferred_element_type=F32)


def _split3(x):
    h1 = x.astype(BF16)
    r1 = x - h1.astype(F32)
    h2 = r1.astype(BF16)
    h3 = (r1 - h2.astype(F32)).astype(BF16)
    return h1, h2, h3


def _split2(x):
    h1 = x.astype(BF16)
    h2 = (x - h1.astype(F32)).astype(BF16)
    return h1, h2


def _group_sum(x, bd):
    w = bd.shape[0]
    hi, lo = _split2(x)
    outs = []
    for c in range(x.shape[1] // w):
        sl = slice(c * w, (c + 1) * w)
        outs.append(_dot(hi[:, sl], bd) + _dot(lo[:, sl], bd))
    return outs[0] if len(outs) == 1 else jnp.concatenate(outs, axis=1)


def _cumsum_rows(tri, x):
    h1, h2, h3 = _split3(x)
    return _dot(tri, h1) + _dot(tri, h2) + _dot(tri, h3)


def _sigmoid(x):
    return 1.0 / (1.0 + jnp.exp(-x))


def _softplus(x):
    return jnp.maximum(x, 0.0) + jnp.log1p(jnp.exp(-jnp.abs(x)))


def _rms(x, g):
    ms = jnp.mean(x * x, axis=-1, keepdims=True)
    return x * lax.rsqrt(ms + NORM_EPS) * g


def _params(sem):
    return pltpu.CompilerParams(dimension_semantics=sem, vmem_limit_bytes=VMEM_LIMIT)


def _in_proj_kernel(x_ref, g_ref, w_ref, o_ref, xn_ref):
    @pl.when(pl.program_id(1) == 0)
    def _():
        xn_ref[...] = _rms(x_ref[...], g_ref[...]).astype(BF16)

    o_ref[...] = _dot(xn_ref[...], w_ref[...])


def _in_proj(h, g, w, *, tm, tn):
    t, d = h.shape
    n = w.shape[1]
    return pl.pallas_call(
        _in_proj_kernel,
        out_shape=jax.ShapeDtypeStruct((t, n), F32),
        grid=(t // tm, n // tn),
        in_specs=[
            pl.BlockSpec((tm, d), lambda i, j: (i, 0)),
            pl.BlockSpec((1, d), lambda i, j: (0, 0)),
            pl.BlockSpec((d, tn), lambda i, j: (0, j)),
        ],
        out_specs=pl.BlockSpec((tm, tn), lambda i, j: (i, j)),
        scratch_shapes=[pltpu.VMEM((tm, d), BF16)],
        compiler_params=_params(("parallel", "arbitrary")),
        name="in_proj",
    )(h, g, w)


def _rwkv_prep_kernel(zr_ref, zl_ref, pr_ref, pl_ref, mur_ref, mul_ref, w0_ref, w2_ref, a0_ref,
                      a2_ref, g2_ref, kkw_ref, kaw_ref, bd_ref,
                      r_out, ld_out, kp_out, v_out, kk_out, b_out, g_out, *, blocks_per_seq):
    first = (pl.program_id(0) % blocks_per_seq) == 0

    def shift(z, prev8, mu):
        last = jnp.where(first, 0.0, prev8[7:8, :])
        rolled = pltpu.roll(z, 1, 0)
        row = lax.broadcasted_iota(jnp.int32, z.shape, 0)
        prev = jnp.where(row == 0, last, rolled)
        return z + (prev - z) * mu

    zs = shift(zr_ref[...], pr_ref[...], mur_ref[...])
    zl = shift(zl_ref[...], pl_ref[...], mul_ref[...])
    r = zs[:, :D_RWKV]
    k = zs[:, D_RWKV:2 * D_RWKV]
    v = zs[:, 2 * D_RWKV:]

    lw = w0_ref[...] + _dot(jnp.tanh(zl).astype(BF16), w2_ref[...])
    w_log = -_softplus(-lw) - 0.5
    ld = -jnp.exp(w_log)
    a = _sigmoid(a0_ref[...] + _dot(zl.astype(BF16), a2_ref[...]))
    g = _dot(_sigmoid(zl).astype(BF16), g2_ref[...])

    kkr = k * kkw_ref[...]
    ss = _group_sum(kkr * kkr, bd_ref[...])
    kk = kkr / jnp.maximum(jnp.sqrt(ss), 1e-12)
    kp = k * (1.0 + (a - 1.0) * kaw_ref[...])

    r_out[...] = r
    ld_out[...] = ld
    kp_out[...] = kp
    v_out[...] = v
    kk_out[...] = kk
    b_out[...] = kk * a
    g_out[...] = g


def _rwkv_prep(z, seq, mu_r, mu_l, w0, w2p, a0, a2p, g2p, k_k, k_a, bd, *, tm):
    t = z.shape[0]
    c_lora = (3 * D_RWKV + 3 * D_FOX) // N_LORA_PAD
    rows8 = tm // 8
    row = lambda i: (0, 0)
    prev = lambda i: (jnp.maximum(i * rows8 - 1, 0), 0)
    out = jax.ShapeDtypeStruct((t, D_RWKV), F32)
    return pl.pallas_call(
        functools.partial(_rwkv_prep_kernel, blocks_per_seq=seq // tm),
        out_shape=[out] * 7,
        grid=(t // tm,),
        in_specs=[
            pl.BlockSpec((tm, 3 * D_RWKV), lambda i: (i, 0)),
            pl.BlockSpec((tm, N_LORA_PAD), lambda i: (i, c_lora)),
            pl.BlockSpec((8, 3 * D_RWKV), prev),
            pl.BlockSpec((8, N_LORA_PAD), lambda i: (jnp.maximum(i * rows8 - 1, 0), c_lora)),
            pl.BlockSpec((1, 3 * D_RWKV), row),
            pl.BlockSpec((1, N_LORA_PAD), row),
            pl.BlockSpec((1, D_RWKV), row),
            pl.BlockSpec((N_LORA_PAD, D_RWKV), row),
            pl.BlockSpec((1, D_RWKV), row),
            pl.BlockSpec((N_LORA_PAD, D_RWKV), row),
            pl.BlockSpec((N_LORA_PAD, D_RWKV), row),
            pl.BlockSpec((1, D_RWKV), row),
            pl.BlockSpec((1, D_RWKV), row),
            pl.BlockSpec(bd.shape, row),
        ],
        out_specs=[pl.BlockSpec((tm, D_RWKV), lambda i: (i, 0))] * 7,
        compiler_params=_params(("parallel",)),
        name="rwkv_prep",
    )(z, z, z, z, mu_r, mu_l, w0, w2p, a0, a2p, g2p, k_k, k_a, bd)


def _stack2(q, m0):
    zero = jnp.zeros_like(q)
    return jnp.concatenate([jnp.where(m0, q, zero), jnp.where(m0, zero, q)], axis=0)


def _scan_kernel(r_ref, ld_ref, kp_ref, v_ref, kk_ref, b_ref, g_ref, gnw_ref, gnb_ref, rk_ref,
                 bd_ref, y_ref, s_ref, *, nchunk):
    L = CHUNK

    @pl.when(pl.program_id(2) == 0)
    def _():
        s_ref[...] = jnp.zeros_like(s_ref)

    lane = lax.broadcasted_iota(jnp.int32, (1, LANES), 1)
    m0 = lane < HEAD_DIM
    row = lax.broadcasted_iota(jnp.int32, (L, LANES), 0)
    col = lax.broadcasted_iota(jnp.int32, (L, LANES), 1) % HEAD_DIM
    strict = row > col
    incl = row >= col
    eye = (row == col).astype(F32)
    tri = (lax.broadcasted_iota(jnp.int32, (L, L), 0)
           >= lax.broadcasted_iota(jnp.int32, (L, L), 1)).astype(BF16)
    bd = bd_ref[...]

    def pairmm(p, q):
        return _dot(p.astype(BF16), _stack2(q, m0).astype(BF16))

    s = s_ref[...]
    for c in range(nchunk):
        sl = slice(c * L, (c + 1) * L)
        r = r_ref[sl, :]
        ld = ld_ref[sl, :]
        kp = kp_ref[sl, :]
        v = v_ref[sl, :]
        kk = kk_ref[sl, :]
        b = b_ref[sl, :]

        cum = _cumsum_rows(tri, ld)
        w_inv = jnp.exp(-cum)
        w_last = jnp.exp(cum[L - 1:L, :])
        abar = -kk * jnp.exp(cum - ld)
        rbar = r * jnp.exp(cum)
        btil = b * w_inv
        ktil = kp * w_inv
        bhat = btil * w_last
        khat = ktil * w_last

        x = jnp.concatenate([abar, rbar], axis=0).astype(BF16)
        yst = jnp.concatenate([_stack2(btil, m0), _stack2(ktil, m0)], axis=0).astype(BF16)
        a_all = _dot_nt(x, yst)
        zero = jnp.zeros((L, LANES), F32)
        n = jnp.where(strict, a_all[:L, :LANES], zero)
        a_ak = jnp.where(strict, a_all[:L, LANES:], zero)
        a_rb = jnp.where(incl, a_all[L:, :LANES], zero)
        a_rk = jnp.where(incl, a_all[L:, LANES:], zero)

        tinv = eye + n
        nk = n
        for _ in range(5):
            nk = pairmm(nk, nk)
            tinv = tinv + pairmm(tinv, nk)

        xs = _dot_nt(x, _stack2(s, m0).astype(BF16))
        av = _dot(jnp.concatenate([a_ak, a_rk], axis=0).astype(BF16),
                  _stack2(v, m0).astype(BF16))
        u = pairmm(tinv, xs[:L] + av[:L])
        y = xs[L:] + pairmm(a_rb, u) + av[L:]
        full = _dot_tn(jnp.concatenate([u, v], axis=0).astype(BF16),
                       jnp.concatenate([bhat, khat], axis=0).astype(BF16))
        s = s * w_last + jnp.where(m0, full[:HEAD_DIM], full[HEAD_DIM:])

        mu = _group_sum(y, bd) * (1.0 / HEAD_DIM)
        d = y - mu
        var = _group_sum(d * d, bd) * (1.0 / HEAD_DIM)
        yn = d * lax.rsqrt(var + GN_EPS) * gnw_ref[...] + gnb_ref[...]
        bonus = _group_sum(r * kp * rk_ref[...], bd) * v
        y_ref[sl, :] = ((yn + bonus) * g_ref[sl, :]).astype(y_ref.dtype)

    s_ref[...] = s


def _rwkv_scan(feats, g, gn_w, gn_b, r_k, bd128, batch, seq, *, tc):
    npair = D_RWKV // LANES
    tok = pl.BlockSpec((None, tc, LANES), lambda b, p, c: (b, c, p))
    par = pl.BlockSpec((1, LANES), lambda b, p, c: (0, p))
    feats = [f.reshape(batch, seq, D_RWKV) for f in feats]
    return pl.pallas_call(
        functools.partial(_scan_kernel, nchunk=tc // CHUNK),
        out_shape=jax.ShapeDtypeStruct((batch, seq, D_RWKV), BF16),
        grid=(batch, npair, seq // tc),
        in_specs=[tok] * 7 + [par] * 3 + [pl.BlockSpec((LANES, LANES), lambda b, p, c: (0, 0))],
        out_specs=tok,
        scratch_shapes=[pltpu.VMEM((HEAD_DIM, LANES), F32)],
        compiler_params=_params(("parallel", "parallel", "arbitrary")),
        name="rwkv_scan",
    )(*feats, g.reshape(batch, seq, D_RWKV), gn_w, gn_b, r_k, bd128)


def _fox_prep_kernel(q_ref, k_ref, v_ref, ff_ref, gq_ref, gk_ref, bf_ref, bd_ref, sel_ref, one_ref,
                     qt_ref, ka_ref, vt_ref, ct_ref, carry_ref, *, blocks_per_seq):
    @pl.when((pl.program_id(0) % blocks_per_seq) == 0)
    def _():
        carry_ref[...] = jnp.zeros_like(carry_ref)

    bd = bd_ref[...]
    q = q_ref[...]
    k = k_ref[...]
    msq = _group_sum(q * q, bd) * (1.0 / HEAD_DIM)
    msk = _group_sum(k * k, bd) * (1.0 / HEAD_DIM)
    qn = q * lax.rsqrt(msq + NORM_EPS) * gq_ref[...] * (HEAD_DIM ** -0.5 * LOG2E)
    kn = k * lax.rsqrt(msk + NORM_EPS) * gk_ref[...]
    qt_ref[...] = qn.T.astype(BF16)
    vt_ref[...] = v_ref[...].T.astype(BF16)

    tm = q.shape[0]
    log_f = -_softplus(-(ff_ref[...] + bf_ref[...])) * LOG2E
    tri = (lax.broadcasted_iota(jnp.int32, (tm, tm), 0)
           >= lax.broadcasted_iota(jnp.int32, (tm, tm), 1)).astype(BF16)
    c_local = _cumsum_rows(tri, log_f)
    c = c_local + carry_ref[...]
    carry_ref[...] = c[tm - 1:tm, :]
    ct_ref[...] = c.T[:ct_ref.shape[0], :]

    n1, n2, n3 = _split3(-c_local)
    aug = _dot(n1, sel_ref[0]) + _dot(n2, sel_ref[1]) + _dot(n3, sel_ref[2]) + one_ref[...]
    for p in range(D_FOX // LANES):
        src = slice(p * LANES, (p + 1) * LANES)
        ka_ref[:, 2 * p * LANES:(2 * p + 1) * LANES] = kn[:, src].astype(BF16)
        ka_ref[:, (2 * p + 1) * LANES:(2 * p + 2) * LANES] = aug[:, src].astype(BF16)


def _fox_prep(z, seq, gq, gk, bf, bd, sel, one):
    t = z.shape[0]
    tm = TK
    nh = D_FOX // HEAD_DIM
    c_ff = (3 * D_RWKV + 3 * D_FOX + N_LORA_PAD) // N_FF_PAD
    row = lambda i: (0, 0)
    tr = jax.ShapeDtypeStruct((D_FOX, t), BF16)
    return pl.pallas_call(
        functools.partial(_fox_prep_kernel, blocks_per_seq=seq // tm),
        out_shape=[tr, jax.ShapeDtypeStruct((t, 2 * D_FOX), BF16), tr,
                   jax.ShapeDtypeStruct((nh, t), F32)],
        grid=(t // tm,),
        in_specs=[
            pl.BlockSpec((tm, D_FOX), lambda i: (i, 3)),
            pl.BlockSpec((tm, D_FOX), lambda i: (i, 4)),
            pl.BlockSpec((tm, D_FOX), lambda i: (i, 5)),
            pl.BlockSpec((tm, N_FF_PAD), lambda i: (i, c_ff)),
            pl.BlockSpec((1, D_FOX), row),
            pl.BlockSpec((1, D_FOX), row),
            pl.BlockSpec((1, N_FF_PAD), row),
            pl.BlockSpec(bd.shape, row),
            pl.BlockSpec(sel.shape, lambda i: (0, 0, 0)),
            pl.BlockSpec((1, D_FOX), row),
        ],
        out_specs=[pl.BlockSpec((D_FOX, tm), lambda i: (0, i)),
                   pl.BlockSpec((tm, 2 * D_FOX), lambda i: (i, 0)),
                   pl.BlockSpec((D_FOX, tm), lambda i: (0, i)),
                   pl.BlockSpec((nh, tm), lambda i: (0, i))],
        scratch_shapes=[pltpu.VMEM((1, N_FF_PAD), F32)],
        compiler_params=_params(("arbitrary",)),
        name="fox_prep",
    )(z, z, z, z, gq, gk, bf, bd, sel, one)


def _attn_kernel(thr_ref, cref_ref, qt_ref, ka_ref, vt_ref, ct_ref, o_ref, acc_ref, *, nblk):
    b = pl.program_id(0)
    p = pl.program_id(1)
    i = pl.program_id(2)
    nh = ct_ref.shape[0]
    tq = qt_ref.shape[1]
    qoff = pl.multiple_of(i * tq, tq)

    row_d = lax.broadcasted_iota(jnp.int32, (LANES, 1), 0)
    qt = qt_ref[...]
    zq = jnp.zeros_like(qt)
    qte = (jnp.where(row_d < HEAD_DIM, qt, zq), jnp.where(row_d < HEAD_DIM, zq, qt))
    rs = lax.broadcasted_iota(jnp.int32, (16, tq), 0)
    zpad = jnp.zeros((LANES - 16, tq), BF16)
    base = tuple((b * nh + 2 * p + e) * (nblk + 1) for e in range(2))
    crow = tuple(ct_ref[pl.ds(2 * p + e, 1), pl.ds(qoff, tq)] for e in range(2))
    ones3 = tuple(((rs >= AUG * e) & (rs < AUG * e + 3)).astype(F32) for e in range(2))

    def logits(j, off, e):
        d = crow[e] - cref_ref[base[e] + j]
        d1 = d.astype(BF16).astype(F32)
        r1 = d - d1
        d2 = r1.astype(BF16).astype(F32)
        d3 = r1 - d2
        strip = jnp.where(rs == AUG * e + 3, d1,
                          jnp.where(rs == AUG * e + 4, d2,
                                    jnp.where(rs == AUG * e + 5, d3, ones3[e])))
        rhs = jnp.concatenate([qte[e], strip.astype(BF16), zpad], axis=0)
        return _dot(ka_ref[pl.ds(off, TK), :], rhs)

    def update(st, off, e, carry):
        m, l = carry
        m_new = jnp.maximum(m, jnp.max(st, axis=0, keepdims=True))
        alpha = jnp.exp2(m - m_new)
        pt = jnp.exp2(st - m_new)
        l = alpha * l + jnp.sum(pt, axis=0, keepdims=True)
        vte = vt_ref[pl.ds(HEAD_DIM * e, HEAD_DIM), pl.ds(off, TK)]
        acc_ref[e] = alpha * acc_ref[e] + _dot(vte, pt.astype(BF16))
        return m_new, l

    acc_ref[...] = jnp.zeros_like(acc_ref)
    init = (jnp.full((1, tq), -jnp.inf, F32), jnp.zeros((1, tq), F32))

    tt = lax.broadcasted_iota(jnp.int32, (TK, tq), 0)
    rr = lax.broadcasted_iota(jnp.int32, (TK, tq), 1)
    carry = []
    for e in range(2):
        st = jnp.where(tt <= rr, logits(i, qoff, e), -jnp.inf)
        carry.append(update(st, qoff, e, init))
    carry = tuple(carry)

    thr = thr_ref[0]

    def live_blocks(e):
        ci = cref_ref[base[e] + i]
        return lax.while_loop(
            lambda n: jnp.logical_and(n < i, ci - cref_ref[base[e] + i - n] >= thr),
            lambda n: n + 1, jnp.int32(0))

    nlive = jnp.maximum(live_blocks(0), live_blocks(1))

    def body(n, carry):
        j = i - 1 - n
        off = pl.multiple_of(j * TK, TK)
        sts = tuple(logits(j, off, e) for e in range(2))
        return tuple(update(sts[e], off, e, carry[e]) for e in range(2))

    carry = lax.fori_loop(0, nlive, body, carry)
    (_, l0), (_, l1) = carry
    ot = jnp.concatenate([acc_ref[0] / l0, acc_ref[1] / l1], axis=0)
    o_ref[...] = ot.T.astype(o_ref.dtype)


def _fox_attention(thr, cref, qt, ka, vt, ct, batch, seq):
    tq = TK
    npair = D_FOX // LANES
    nh = D_FOX // HEAD_DIM
    nq = seq // tq
    smem = pl.BlockSpec(memory_space=pltpu.SMEM)
    return pl.pallas_call(
        functools.partial(_attn_kernel, nblk=seq // TK),
        out_shape=jax.ShapeDtypeStruct((batch, seq, D_FOX), BF16),
        grid=(batch, npair, nq),
        in_specs=[
            smem, smem,
            pl.BlockSpec((LANES, tq), lambda b, p, i: (p, b * nq + i)),
            pl.BlockSpec((seq, 2 * LANES), lambda b, p, i: (b, p)),
            pl.BlockSpec((LANES, seq), lambda b, p, i: (p, b)),
            pl.BlockSpec((nh, seq), lambda b, p, i: (0, b)),
        ],
        out_specs=pl.BlockSpec((None, tq, LANES), lambda b, p, i: (b, i, p)),
        scratch_shapes=[pltpu.VMEM((2, HEAD_DIM, tq), F32)],
        compiler_params=_params(("parallel", "parallel", "arbitrary")),
        name="fox_attn",
    )(thr, cref, qt, ka, vt, ct)


def _out_proj_kernel(h_ref, y1_ref, y2_ref, w1_ref, w2_ref, o_ref):
    o_ref[...] = h_ref[...] + _dot(y1_ref[...], w1_ref[...]) + _dot(y2_ref[...], w2_ref[...])


def _out_proj(h, y1, y2, w1, w2, *, tm, tn):
    t, d = h.shape
    return pl.pallas_call(
        _out_proj_kernel,
        out_shape=jax.ShapeDtypeStruct((t, d), F32),
        grid=(t // tm, d // tn),
        in_specs=[
            pl.BlockSpec((tm, tn), lambda i, j: (i, j)),
            pl.BlockSpec((tm, y1.shape[1]), lambda i, j: (i, 0)),
            pl.BlockSpec((tm, y2.shape[1]), lambda i, j: (i, 0)),
            pl.BlockSpec((w1.shape[0], tn), lambda i, j: (0, j)),
            pl.BlockSpec((w2.shape[0], tn), lambda i, j: (0, j)),
        ],
        out_specs=pl.BlockSpec((tm, tn), lambda i, j: (i, j)),
        compiler_params=_params(("parallel", "parallel")),
        name="out_proj",
    )(h, y1, y2, w1, w2)


def _mlp_kernel(x_ref, g_ref, w1_ref, w2_ref, o_ref, xn_ref):
    @pl.when(pl.program_id(1) == 0)
    def _():
        x = x_ref[...]
        xn_ref[...] = _rms(x, g_ref[...]).astype(BF16)
        o_ref[...] = x

    hid = jnp.maximum(_dot(xn_ref[...], w1_ref[...]), 0.0)
    o_ref[...] += _dot((hid * hid).astype(BF16), w2_ref[...])


def _mlp(h, g, w1, w2, *, tm, tf):
    t, d = h.shape
    f = w1.shape[1]
    return pl.pallas_call(
        _mlp_kernel,
        out_shape=jax.ShapeDtypeStruct((t, d), F32),
        grid=(t // tm, f // tf),
        in_specs=[
            pl.BlockSpec((tm, d), lambda i, j: (i, 0)),
            pl.BlockSpec((1, d), lambda i, j: (0, 0)),
            pl.BlockSpec((d, tf), lambda i, j: (0, j)),
            pl.BlockSpec((tf, d), lambda i, j: (j, 0)),
        ],
        out_specs=pl.BlockSpec((tm, d), lambda i, j: (i, 0)),
        scratch_shapes=[pltpu.VMEM((tm, d), BF16)],
        compiler_params=_params(("parallel", "arbitrary")),
        name="mlp",
    )(h, g, w1, w2)


def _ple_kernel(x_ref, xc_ref, g_ref, p_ref, wg_ref, wp_ref, o_ref, xn_ref):
    @pl.when(pl.program_id(1) == 0)
    def _():
        xn_ref[...] = _rms(x_ref[...], g_ref[...]).astype(BF16)

    gate = _sigmoid(_dot(xn_ref[...], wg_ref[...]))
    o_ref[...] = xc_ref[...] + gate * _dot(p_ref[...].astype(BF16), wp_ref[...])


def _ple(h, g, p, wg, wp, *, tm, tn):
    t, d = h.shape
    dp = p.shape[1]
    return pl.pallas_call(
        _ple_kernel,
        out_shape=jax.ShapeDtypeStruct((t, d), F32),
        grid=(t // tm, d // tn),
        in_specs=[
            pl.BlockSpec((tm, d), lambda i, j: (i, 0)),
            pl.BlockSpec((tm, tn), lambda i, j: (i, j)),
            pl.BlockSpec((1, d), lambda i, j: (0, 0)),
            pl.BlockSpec((tm, dp), lambda i, j: (i, 0)),
            pl.BlockSpec((d, tn), lambda i, j: (0, j)),
            pl.BlockSpec((dp, tn), lambda i, j: (0, j)),
        ],
        out_specs=pl.BlockSpec((tm, tn), lambda i, j: (i, j)),
        scratch_shapes=[pltpu.VMEM((tm, d), BF16)],
        compiler_params=_params(("parallel", "arbitrary")),
        name="ple",
    )(h, h, g, p, wg, wp)


def _final_norm_kernel(x_ref, g_ref, o_ref):
    o_ref[...] = _rms(x_ref[...], g_ref[...])


def _final_norm(h, g, *, tm):
    t, d = h.shape
    return pl.pallas_call(
        _final_norm_kernel,
        out_shape=jax.ShapeDtypeStruct((t, d), F32),
        grid=(t // tm,),
        in_specs=[pl.BlockSpec((tm, d), lambda i: (i, 0)), pl.BlockSpec((1, d), lambda i: (0, 0))],
        out_specs=pl.BlockSpec((tm, d), lambda i: (i, 0)),
        compiler_params=_params(("parallel",)),
        name="final_norm",
    )(h, g)


def _block_diag_ones(n):
    idx = jnp.arange(n) // HEAD_DIM
    return (idx[:, None] == idx[None, :]).astype(BF16)


def _pad_cols(w, n):
    return jnp.pad(w, ((0, 0), (0, n - w.shape[1])))


def _pack_w_in(w):
    n_shift = 3 * D_RWKV + N_LORA
    rkv = w[:, :3 * D_RWKV]
    lora = _pad_cols(w[:, 3 * D_RWKV:n_shift], N_LORA_PAD)
    fox = w[:, n_shift:n_shift + 3 * D_FOX]
    ff = _pad_cols(w[:, n_shift + 3 * D_FOX:], N_FF_PAD)
    return jnp.concatenate([rkv, fox, lora, ff], axis=1).astype(BF16)


def _bias_lane_tables():
    nh = D_FOX // HEAD_DIM
    h = jnp.arange(nh)
    lane = (h // 2) * LANES + (h % 2) * AUG
    sel = jnp.zeros((3, N_FF_PAD, D_FOX), F32)
    one = jnp.zeros((1, D_FOX), F32)
    for k in range(3):
        sel = sel.at[k, h, lane + k].set(1.0)
        one = one.at[0, lane + 3 + k].set(1.0)
    return sel.astype(BF16), one


def _skip_threshold(gq, gk):
    smax = 1.02 * 8.0 * jnp.max(jnp.abs(gq)) * jnp.max(jnp.abs(gk)) * LOG2E
    return (-(EXP2_UNDERFLOW + 2.0 * smax)).reshape(1).astype(F32)


def _block_starts(ct, batch, seq):
    nh = ct.shape[0]
    ends = ct.reshape(nh, batch, seq // TK, TK)[..., TK - 1]
    starts = jnp.pad(ends, ((0, 0), (0, 0), (1, 0)))
    return starts.transpose(1, 0, 2).reshape(-1)


def _lora_rows(w, start):
    return jnp.pad(w, ((start, N_LORA_PAD - start - w.shape[0]), (0, 0))).astype(BF16)


def kernel(x, p, norm_mix, w_in, mu_shift, w0, w2, a0, a2, g2, k_k, k_a, r_k, gn_w, gn_b, b_f, q_norm, k_norm, w_out, norm_mlp, w_ff1, w_ff2, norm_ple, w_ple_gate, w_ple_proj, norm_final):
    batch, seq, d = x.shape
    depth = w_in.shape[0]
    t = batch * seq
    nh_fox = D_FOX // HEAD_DIM
    tm = min(512, seq)
    sel, one = _bias_lane_tables()
    bd256 = _block_diag_ones(256)
    bd128 = _block_diag_ones(LANES)
    row = lambda a: a.reshape(1, -1)

    h = x.reshape(t, d)
    for i in range(depth):
        z = _in_proj(h, row(norm_mix[i]), _pack_w_in(w_in[i]), tm=tm, tn=512)

        mu = mu_shift[i]
        feats = _rwkv_prep(
            z, seq, row(mu[:3 * D_RWKV]), row(jnp.pad(mu[3 * D_RWKV:], (0, N_LORA_PAD - N_LORA))),
            row(w0[i]), _lora_rows(w2[i], 0), row(a0[i]), _lora_rows(a2[i], 64),
            _lora_rows(g2[i], 128), row(k_k[i]), row(k_a[i]), bd256, tm=min(256, seq))
        y_rw = _rwkv_scan(feats[:6], feats[6], row(gn_w[i]), row(gn_b[i]), row(r_k[i]), bd128,
                          batch, seq, tc=min(256, seq))

        qt, ka, vt, ct = _fox_prep(
            z, seq, row(jnp.tile(q_norm[i], nh_fox)), row(jnp.tile(k_norm[i], nh_fox)),
            row(jnp.pad(b_f[i], (0, N_FF_PAD - nh_fox))), bd256, sel, one)
        y_fox = _fox_attention(_skip_threshold(q_norm[i], k_norm[i]), _block_starts(ct, batch, seq),
                               qt, ka, vt, ct, batch, seq)

        wo = w_out[i].astype(BF16)
        h = _out_proj(h, y_rw.reshape(t, D_RWKV), y_fox.reshape(t, D_FOX), wo[:D_RWKV], wo[D_RWKV:],
                      tm=tm, tn=512)
        h = _mlp(h, row(norm_mlp[i]), w_ff1[i].astype(BF16), w_ff2[i].astype(BF16), tm=tm, tf=512)
        h = _ple(h, row(norm_ple[i]), p[i].reshape(t, -1), w_ple_gate[i].astype(BF16),
                 w_ple_proj[i].astype(BF16), tm=tm, tn=512)
    return _final_norm(h, row(norm_final), tm=tm).reshape(batch, seq, d)
```

```python
import functools

import jax
import jax.numpy as jnp
from jax import lax
from jax.experimental import pallas as pl
from jax.experimental.pallas import tpu as pltpu

F32 = jnp.float32
BF16 = jnp.bfloat16

HEAD_DIM = 64
D_RWKV = 1024
D_FOX = 1024
N_LORA = 288
N_LORA_PAD = 384
N_FF_PAD = 128
NORM_EPS = 1e-6
GN_EPS = 64e-5
CHUNK = 64
LANES = 128
TK = 256
AUG = 8
LOG2E = 1.4426950408889634
EXP2_UNDERFLOW = 152.0
UNROLL = 4
VMEM_LIMIT = 56 * 1024 * 1024


def _dot(a, b):
    return jnp.dot(a, b, preferred_element_type=F32)


def _dot_nt(a, b):
    return lax.dot_general(a, b, (((1,), (1,)), ((), ())), preferred_element_type=F32)


def _dot_tn(a, b):
    return lax.dot_general(a, b, (((0,), (0,)), ((), ())), preferred_element_type=F32)


def _split3(x):
    h1 = x.astype(BF16)
    r1 = x - h1.astype(F32)
    h2 = r1.astype(BF16)
    h3 = (r1 - h2.astype(F32)).astype(BF16)
    return h1, h2, h3


def _split2(x):
    h1 = x.astype(BF16)
    h2 = (x - h1.astype(F32)).astype(BF16)
    return h1, h2


def _group_sum(x, bd):
    w = bd.shape[0]
    hi, lo = _split2(x)
    outs = []
    for c in range(x.shape[1] // w):
        sl = slice(c * w, (c + 1) * w)
        outs.append(_dot(hi[:, sl], bd) + _dot(lo[:, sl], bd))
    return outs[0] if len(outs) == 1 else jnp.concatenate(outs, axis=1)


def _group_sums(xs, bd):
    rows = xs[0].shape[0]
    parts = []
    for x in xs:
        parts.extend(_split2(x))
    out = _dot(jnp.concatenate(parts, axis=0), bd)
    return [out[2 * k * rows:(2 * k + 1) * rows] + out[(2 * k + 1) * rows:(2 * k + 2) * rows]
            for k in range(len(xs))]


def _cumsum_rows(tri, x):
    h1, h2, h3 = _split3(x)
    return _dot(tri, h1) + _dot(tri, h2) + _dot(tri, h3)


def _sigmoid(x):
    return 1.0 / (1.0 + jnp.exp(-x))


def _softplus(x):
    return jnp.maximum(x, 0.0) + jnp.log1p(jnp.exp(-jnp.abs(x)))


def _rms(x, g):
    ms = jnp.mean(x * x, axis=-1, keepdims=True)
    return x * lax.rsqrt(ms + NORM_EPS) * g


def _params(sem):
    return pltpu.CompilerParams(dimension_semantics=sem, vmem_limit_bytes=VMEM_LIMIT)


def _in_proj_kernel(x_ref, g_ref, w_ref, o_ref, xn_ref):
    @pl.when(pl.program_id(1) == 0)
    def _():
        xn_ref[...] = _rms(x_ref[...], g_ref[...]).astype(BF16)

    o_ref[...] = _dot(xn_ref[...], w_ref[...])


def _in_proj(h, g, w, *, tm, tn):
    t, d = h.shape
    n = w.shape[1]
    return pl.pallas_call(
        _in_proj_kernel,
        out_shape=jax.ShapeDtypeStruct((t, n), F32),
        grid=(t // tm, n // tn),
        in_specs=[
            pl.BlockSpec((tm, d), lambda i, j: (i, 0)),
            pl.BlockSpec((1, d), lambda i, j: (0, 0)),
            pl.BlockSpec((d, tn), lambda i, j: (0, j)),
        ],
        out_specs=pl.BlockSpec((tm, tn), lambda i, j: (i, j)),
        scratch_shapes=[pltpu.VMEM((tm, d), BF16)],
        compiler_params=_params(("parallel", "arbitrary")),
        name="in_proj",
    )(h, g, w)


def _rwkv_prep_kernel(zr_ref, zl_ref, pr_ref, pl_ref, mur_ref, mul_ref, w0_ref, w2_ref, a0_ref,
                      a2_ref, g2_ref, kkw_ref, kaw_ref, bd_ref,
                      r_out, ld_out, kp_out, v_out, kk_out, b_out, g_out, *, blocks_per_seq):
    first = (pl.program_id(0) % blocks_per_seq) == 0

    def shift(z, prev8, mu):
        last = jnp.where(first, 0.0, prev8[7:8, :])
        rolled = pltpu.roll(z, 1, 0)
        row = lax.broadcasted_iota(jnp.int32, z.shape, 0)
        prev = jnp.where(row == 0, last, rolled)
        return z + (prev - z) * mu

    zs = shift(zr_ref[...], pr_ref[...], mur_ref[...])
    zl = shift(zl_ref[...], pl_ref[...], mul_ref[...])
    r = zs[:, :D_RWKV]
    k = zs[:, D_RWKV:2 * D_RWKV]
    v = zs[:, 2 * D_RWKV:]

    lw = w0_ref[...] + _dot(jnp.tanh(zl).astype(BF16), w2_ref[...])
    w_log = -_softplus(-lw) - 0.5
    ld = -jnp.exp(w_log)
    a = _sigmoid(a0_ref[...] + _dot(zl.astype(BF16), a2_ref[...]))
    g = _dot(_sigmoid(zl).astype(BF16), g2_ref[...])

    kkr = k * kkw_ref[...]
    ss = _group_sum(kkr * kkr, bd_ref[...])
    kk = kkr / jnp.maximum(jnp.sqrt(ss), 1e-12)
    kp = k * (1.0 + (a - 1.0) * kaw_ref[...])

    r_out[...] = r
    ld_out[...] = ld
    kp_out[...] = kp
    v_out[...] = v
    kk_out[...] = kk
    b_out[...] = kk * a
    g_out[...] = g


def _rwkv_prep(z, seq, mu_r, mu_l, w0, w2p, a0, a2p, g2p, k_k, k_a, bd, *, tm):
    t = z.shape[0]
    c_lora = (3 * D_RWKV + 3 * D_FOX) // N_LORA_PAD
    rows8 = tm // 8
    row = lambda i: (0, 0)
    prev = lambda i: (jnp.maximum(i * rows8 - 1, 0), 0)
    out = jax.ShapeDtypeStruct((t, D_RWKV), F32)
    return pl.pallas_call(
        functools.partial(_rwkv_prep_kernel, blocks_per_seq=seq // tm),
        out_shape=[out] * 7,
        grid=(t // tm,),
        in_specs=[
            pl.BlockSpec((tm, 3 * D_RWKV), lambda i: (i, 0)),
            pl.BlockSpec((tm, N_LORA_PAD), lambda i: (i, c_lora)),
            pl.BlockSpec((8, 3 * D_RWKV), prev),
            pl.BlockSpec((8, N_LORA_PAD), lambda i: (jnp.maximum(i * rows8 - 1, 0), c_lora)),
            pl.BlockSpec((1, 3 * D_RWKV), row),
            pl.BlockSpec((1, N_LORA_PAD), row),
            pl.BlockSpec((1, D_RWKV), row),
            pl.BlockSpec((N_LORA_PAD, D_RWKV), row),
            pl.BlockSpec((1, D_RWKV), row),
            pl.BlockSpec((N_LORA_PAD, D_RWKV), row),
            pl.BlockSpec((N_LORA_PAD, D_RWKV), row),
            pl.BlockSpec((1, D_RWKV), row),
            pl.BlockSpec((1, D_RWKV), row),
            pl.BlockSpec(bd.shape, row),
        ],
        out_specs=[pl.BlockSpec((tm, D_RWKV), lambda i: (i, 0))] * 7,
        compiler_params=_params(("parallel",)),
        name="rwkv_prep",
    )(z, z, z, z, mu_r, mu_l, w0, w2p, a0, a2p, g2p, k_k, k_a, bd)


def _stack_heads(q, masks):
    zero = jnp.zeros_like(q)
    return jnp.concatenate([jnp.where(m, q, zero) for m in masks], axis=0).astype(BF16)


def _diag_blocks(full, masks):
    out = full[:HEAD_DIM]
    for g in range(1, len(masks)):
        out = jnp.where(masks[g], full[g * HEAD_DIM:(g + 1) * HEAD_DIM], out)
    return out


def _scan_kernel(r_ref, ld_ref, kp_ref, v_ref, kk_ref, b_ref, g_ref, gnw_ref, gnb_ref, rk_ref,
                 bd_ref, y_ref, s_ref, *, nchunk):
    L = CHUNK
    W = s_ref.shape[1]
    chunks = range(nchunk)

    @pl.when(pl.program_id(2) == 0)
    def _():
        s_ref[...] = jnp.zeros_like(s_ref)

    lane = lax.broadcasted_iota(jnp.int32, (1, W), 1)
    masks = [lane // HEAD_DIM == g for g in range(W // HEAD_DIM)]
    row = lax.broadcasted_iota(jnp.int32, (L, W), 0)
    col = lax.broadcasted_iota(jnp.int32, (L, W), 1) % HEAD_DIM
    strict = row > col
    incl = row >= col
    eye = (row == col).astype(F32)
    tri = (lax.broadcasted_iota(jnp.int32, (L, L), 0)
           >= lax.broadcasted_iota(jnp.int32, (L, L), 1)).astype(BF16)
    bd = bd_ref[...]
    zero = jnp.zeros((L, W), F32)

    def stack(q):
        return _stack_heads(q, masks)

    def hmm(p, qst):
        return _dot(p.astype(BF16), qst)

    sl = [slice(c * L, (c + 1) * L) for c in chunks]
    ld = [ld_ref[s, :] for s in sl]
    cum = [_cumsum_rows(tri, x) for x in ld]
    w_last = [jnp.exp(x[L - 1:L, :]) for x in cum]
    w_inv = [jnp.exp(-x) for x in cum]
    abar = [-kk_ref[sl[c], :] * jnp.exp(cum[c] - ld[c]) for c in chunks]
    rbar = [r_ref[sl[c], :] * jnp.exp(cum[c]) for c in chunks]
    btil = [b_ref[sl[c], :] * w_inv[c] for c in chunks]
    ktil = [kp_ref[sl[c], :] * w_inv[c] for c in chunks]
    bhat = [btil[c] * w_last[c] for c in chunks]
    khat = [ktil[c] * w_last[c] for c in chunks]

    a_all = [_dot_nt(jnp.concatenate([abar[c], rbar[c]], axis=0).astype(BF16),
                     jnp.concatenate([stack(btil[c]), stack(ktil[c])], axis=0))
             for c in chunks]
    n = [jnp.where(strict, a[:L, :W], zero) for a in a_all]
    a_ak = [jnp.where(strict, a[:L, W:], zero) for a in a_all]
    a_rb = [jnp.where(incl, a[L:, :W], zero) for a in a_all]
    a_rk = [jnp.where(incl, a[L:, W:], zero) for a in a_all]

    vst = [stack(v_ref[s, :]) for s in sl]
    av = [hmm(jnp.concatenate([a_ak[c], a_rk[c]], axis=0), vst[c]) for c in chunks]

    tinv = [eye + x for x in n]
    nk = [hmm(x, stack(x)) for x in n]
    for _ in range(4):
        both = [hmm(jnp.concatenate([tinv[c], nk[c]], axis=0), stack(nk[c])) for c in chunks]
        tinv = [tinv[c] + both[c][:L] for c in chunks]
        nk = [x[L:] for x in both]
    tinv = [tinv[c] + hmm(tinv[c], stack(nk[c])) for c in chunks]

    at = [hmm(tinv[c], stack(abar[c])) for c in chunks]
    u0 = [hmm(tinv[c], stack(av[c][:L])) for c in chunks]
    rt = [(rbar[c] + hmm(a_rb[c], stack(at[c]))).astype(BF16) for c in chunks]
    y0 = [hmm(a_rb[c], stack(u0[c])) + av[c][L:] for c in chunks]
    pst = [stack(_diag_blocks(_dot_tn(at[c].astype(BF16), bhat[c].astype(BF16)), masks))
           for c in chunks]
    q = [_diag_blocks(_dot_tn(jnp.concatenate([u0[c], v_ref[sl[c], :]], axis=0).astype(BF16),
                              jnp.concatenate([bhat[c], khat[c]], axis=0).astype(BF16)), masks)
         for c in chunks]

    s = s_ref[...]
    ys = []
    for c in chunks:
        ys.append(_dot_nt(rt[c], stack(s)) + y0[c])
        s = s * w_last[c] + hmm(s, pst[c]) + q[c]
    s_ref[...] = s

    sums = _group_sums(ys + [r_ref[s, :] * kp_ref[s, :] * rk_ref[...] for s in sl], bd)
    ds = [ys[c] - sums[c] * (1.0 / HEAD_DIM) for c in chunks]
    var = _group_sums([d * d for d in ds], bd)
    for c in chunks:
        yn = ds[c] * lax.rsqrt(var[c] * (1.0 / HEAD_DIM) + GN_EPS) * gnw_ref[...] + gnb_ref[...]
        bonus = sums[nchunk + c] * v_ref[sl[c], :]
        y_ref[sl[c], :] = ((yn + bonus) * g_ref[sl[c], :]).astype(y_ref.dtype)


def _rwkv_scan(feats, g, gn_w, gn_b, r_k, bd, batch, seq, *, tc):
    gw = bd.shape[0]
    tok = pl.BlockSpec((None, tc, gw), lambda b, p, c: (b, c, p))
    par = pl.BlockSpec((1, gw), lambda b, p, c: (0, p))
    feats = [f.reshape(batch, seq, D_RWKV) for f in feats]
    return pl.pallas_call(
        functools.partial(_scan_kernel, nchunk=tc // CHUNK),
        out_shape=jax.ShapeDtypeStruct((batch, seq, D_RWKV), BF16),
        grid=(batch, D_RWKV // gw, seq // tc),
        in_specs=[tok] * 7 + [par] * 3 + [pl.BlockSpec((gw, gw), lambda b, p, c: (0, 0))],
        out_specs=tok,
        scratch_shapes=[pltpu.VMEM((HEAD_DIM, gw), F32)],
        compiler_params=_params(("parallel", "parallel", "arbitrary")),
        name="rwkv_scan",
    )(*feats, g.reshape(batch, seq, D_RWKV), gn_w, gn_b, r_k, bd)


def _fox_prep_kernel(q_ref, k_ref, v_ref, ff_ref, gq_ref, gk_ref, bf_ref, bd_ref, sel_ref, one_ref,
                     qt_ref, ka_ref, vt_ref, ct_ref, carry_ref, *, blocks_per_seq):
    @pl.when((pl.program_id(0) % blocks_per_seq) == 0)
    def _():
        carry_ref[...] = jnp.zeros_like(carry_ref)

    bd = bd_ref[...]
    q = q_ref[...]
    k = k_ref[...]
    msq = _group_sum(q * q, bd) * (1.0 / HEAD_DIM)
    msk = _group_sum(k * k, bd) * (1.0 / HEAD_DIM)
    qn = q * lax.rsqrt(msq + NORM_EPS) * gq_ref[...] * (HEAD_DIM ** -0.5 * LOG2E)
    kn = k * lax.rsqrt(msk + NORM_EPS) * gk_ref[...]
    qt_ref[...] = qn.T.astype(BF16)
    vt_ref[...] = v_ref[...].T.astype(BF16)

    tm = q.shape[0]
    log_f = -_softplus(-(ff_ref[...] + bf_ref[...])) * LOG2E
    tri = (lax.broadcasted_iota(jnp.int32, (tm, tm), 0)
           >= lax.broadcasted_iota(jnp.int32, (tm, tm), 1)).astype(BF16)
    c_local = _cumsum_rows(tri, log_f)
    c = c_local + carry_ref[...]
    carry_ref[...] = c[tm - 1:tm, :]
    ct_ref[...] = c.T[:ct_ref.shape[0], :]

    n1, n2, n3 = _split3(-c_local)
    aug = _dot(n1, sel_ref[0]) + _dot(n2, sel_ref[1]) + _dot(n3, sel_ref[2]) + one_ref[...]
    for p in range(D_FOX // LANES):
        src = slice(p * LANES, (p + 1) * LANES)
        ka_ref[:, 2 * p * LANES:(2 * p + 1) * LANES] = kn[:, src].astype(BF16)
        ka_ref[:, (2 * p + 1) * LANES:(2 * p + 2) * LANES] = aug[:, src].astype(BF16)


def _fox_prep(z, seq, gq, gk, bf, bd, sel, one):
    t = z.shape[0]
    tm = TK
    nh = D_FOX // HEAD_DIM
    c_ff = (3 * D_RWKV + 3 * D_FOX + N_LORA_PAD) // N_FF_PAD
    row = lambda i: (0, 0)
    tr = jax.ShapeDtypeStruct((D_FOX, t), BF16)
    return pl.pallas_call(
        functools.partial(_fox_prep_kernel, blocks_per_seq=seq // tm),
        out_shape=[tr, jax.ShapeDtypeStruct((t, 2 * D_FOX), BF16), tr,
                   jax.ShapeDtypeStruct((nh, t), F32)],
        grid=(t // tm,),
        in_specs=[
            pl.BlockSpec((tm, D_FOX), lambda i: (i, 3)),
            pl.BlockSpec((tm, D_FOX), lambda i: (i, 4)),
            pl.BlockSpec((tm, D_FOX), lambda i: (i, 5)),
            pl.BlockSpec((tm, N_FF_PAD), lambda i: (i, c_ff)),
            pl.BlockSpec((1, D_FOX), row),
            pl.BlockSpec((1, D_FOX), row),
            pl.BlockSpec((1, N_FF_PAD), row),
            pl.BlockSpec(bd.shape, row),
            pl.BlockSpec(sel.shape, lambda i: (0, 0, 0)),
            pl.BlockSpec((1, D_FOX), row),
        ],
        out_specs=[pl.BlockSpec((D_FOX, tm), lambda i: (0, i)),
                   pl.BlockSpec((tm, 2 * D_FOX), lambda i: (i, 0)),
                   pl.BlockSpec((D_FOX, tm), lambda i: (0, i)),
                   pl.BlockSpec((nh, tm), lambda i: (0, i))],
        scratch_shapes=[pltpu.VMEM((1, N_FF_PAD), F32)],
        compiler_params=_params(("arbitrary",)),
        name="fox_prep",
    )(z, z, z, z, gq, gk, bf, bd, sel, one)


def _attn_kernel(thr_ref, cref_ref, qt_ref, ka_ref, vt_ref, ct_ref, o_ref, acc_ref, pt_ref,
                 stat_ref, *, nblk):
    b = pl.program_id(0)
    p = pl.program_id(1)
    i = pl.program_id(2)
    nh = ct_ref.shape[0]
    tq = qt_ref.shape[1]
    qoff = pl.multiple_of(i * tq, tq)

    row_d = lax.broadcasted_iota(jnp.int32, (LANES, 1), 0)
    qt = qt_ref[...]
    zq = jnp.zeros_like(qt)
    qte = (jnp.where(row_d < HEAD_DIM, qt, zq), jnp.where(row_d < HEAD_DIM, zq, qt))
    rs = lax.broadcasted_iota(jnp.int32, (16, tq), 0)
    zpad = jnp.zeros((LANES - 16, tq), BF16)
    base = tuple((b * nh + 2 * p + e) * (nblk + 1) for e in range(2))
    ci = tuple(cref_ref[base[e] + i] for e in range(2))

    def rhs_operand(e):
        d = ct_ref[pl.ds(2 * p + e, 1), pl.ds(qoff, tq)] - ci[e]
        d1 = d.astype(BF16).astype(F32)
        r1 = d - d1
        d2 = r1.astype(BF16).astype(F32)
        d3 = r1 - d2
        ones3 = ((rs >= AUG * e) & (rs < AUG * e + 3)).astype(F32)
        strip = jnp.where(rs == AUG * e + 3, d1,
                          jnp.where(rs == AUG * e + 4, d2,
                                    jnp.where(rs == AUG * e + 5, d3, ones3)))
        return jnp.concatenate([qte[e], strip.astype(BF16), zpad], axis=0)

    rhs = tuple(rhs_operand(e) for e in range(2))

    def logits(off, e):
        return _dot(ka_ref[pl.ds(off, TK), :], rhs[e])

    def block_shift(j, e):
        return ci[e] - cref_ref[base[e] + j]

    def col_reduce(x, op):
        return op(op(x.reshape(8, x.shape[0] // 8, x.shape[1]), axis=0), axis=0, keepdims=True)

    def softmax_step(st, e, slot, shift):
        m = stat_ref[e, 0:1, :]
        m_new = jnp.maximum(m, col_reduce(st, jnp.max) + shift)
        alpha = jnp.exp2(m - m_new)
        pt = jnp.exp2(st - (m_new - shift))
        pt_ref[slot, e] = pt.astype(BF16)
        stat_ref[e, 0:1, :] = m_new
        stat_ref[e, 1:2, :] = alpha * stat_ref[e, 1:2, :] + col_reduce(pt, jnp.sum)
        stat_ref[e, 2 + slot:3 + slot, :] = alpha

    def pv_step(off, e, slot):
        vte = vt_ref[pl.ds(HEAD_DIM * e, HEAD_DIM), pl.ds(off, TK)]
        acc_ref[e] = (stat_ref[e, 2 + slot:3 + slot, :] * acc_ref[e]
                      + _dot(vte, pt_ref[slot, e]))

    acc_ref[...] = jnp.zeros_like(acc_ref)
    for e in range(2):
        stat_ref[e, 0:1, :] = jnp.full((1, tq), -jnp.inf, F32)
        stat_ref[e, 1:2, :] = jnp.zeros((1, tq), F32)

    tt = lax.broadcasted_iota(jnp.int32, (TK, tq), 0)
    rr = lax.broadcasted_iota(jnp.int32, (TK, tq), 1)
    for e in range(2):
        softmax_step(jnp.where(tt <= rr, logits(qoff, e), -jnp.inf), e, 0, 0.0)

    thr = thr_ref[0]

    def live_blocks(e):
        return lax.while_loop(
            lambda n: jnp.logical_and(n < i, block_shift(i - n, e) >= thr),
            lambda n: n + 1, jnp.int32(0))

    nlive = jnp.maximum(live_blocks(0), live_blocks(1))

    def step(first, nb):
        offs = [pl.multiple_of((first - t) * TK, TK) for t in range(nb)]
        sts = [tuple(logits(offs[t], e) for e in range(2)) for t in range(nb)]
        for e in range(2):
            pv_step(pl.multiple_of(offs[0] + TK, TK), e, 0)
        for t in range(nb):
            slot = (nb - 1 - t) % 2
            for e in range(2):
                softmax_step(sts[t][e], e, slot, block_shift(first - t, e))
            if t < nb - 1:
                for e in range(2):
                    pv_step(offs[t], e, slot)

    rem = nlive % UNROLL
    done = jnp.int32(0)
    for nb in (1, 2):
        @pl.when((rem & nb) != 0)
        def _(nb=nb, done=done):
            step(i - 1 - done, nb)
        done = done + (rem & nb)

    def body(k, carry):
        step(i - 1 - rem - UNROLL * k, UNROLL)
        return carry

    lax.fori_loop(0, nlive // UNROLL, body, 0)
    last = pl.multiple_of((i - nlive) * TK, TK)
    for e in range(2):
        pv_step(last, e, 0)
    ot = jnp.concatenate([acc_ref[e] / stat_ref[e, 1:2, :] for e in range(2)], axis=0)
    o_ref[...] = ot.T.astype(o_ref.dtype)


def _fox_attention(thr, cref, qt, ka, vt, ct, batch, seq):
    tq = TK
    npair = D_FOX // LANES
    nh = D_FOX // HEAD_DIM
    nq = seq // tq
    smem = pl.BlockSpec(memory_space=pltpu.SMEM)
    return pl.pallas_call(
        functools.partial(_attn_kernel, nblk=seq // TK),
        out_shape=jax.ShapeDtypeStruct((batch, seq, D_FOX), BF16),
        grid=(batch, npair, nq),
        in_specs=[
            smem, smem,
            pl.BlockSpec((LANES, tq), lambda b, p, i: (p, b * nq + i)),
            pl.BlockSpec((seq, 2 * LANES), lambda b, p, i: (b, p)),
            pl.BlockSpec((LANES, seq), lambda b, p, i: (p, b)),
            pl.BlockSpec((nh, seq), lambda b, p, i: (0, b)),
        ],
        out_specs=pl.BlockSpec((None, tq, LANES), lambda b, p, i: (b, i, p)),
        scratch_shapes=[pltpu.VMEM((2, HEAD_DIM, tq), F32), pltpu.VMEM((2, 2, TK, tq), BF16),
                        pltpu.VMEM((2, 8, tq), F32)],
        compiler_params=_params(("parallel", "parallel", "arbitrary")),
        name="fox_attn",
    )(thr, cref, qt, ka, vt, ct)


def _out_proj_kernel(h_ref, y1_ref, y2_ref, w1_ref, w2_ref, o_ref):
    o_ref[...] = h_ref[...] + _dot(y1_ref[...], w1_ref[...]) + _dot(y2_ref[...], w2_ref[...])


def _out_proj(h, y1, y2, w1, w2, *, tm, tn):
    t, d = h.shape
    return pl.pallas_call(
        _out_proj_kernel,
        out_shape=jax.ShapeDtypeStruct((t, d), F32),
        grid=(t // tm, d // tn),
        in_specs=[
            pl.BlockSpec((tm, tn), lambda i, j: (i, j)),
            pl.BlockSpec((tm, y1.shape[1]), lambda i, j: (i, 0)),
            pl.BlockSpec((tm, y2.shape[1]), lambda i, j: (i, 0)),
            pl.BlockSpec((w1.shape[0], tn), lambda i, j: (0, j)),
            pl.BlockSpec((w2.shape[0], tn), lambda i, j: (0, j)),
        ],
        out_specs=pl.BlockSpec((tm, tn), lambda i, j: (i, j)),
        compiler_params=_params(("parallel", "parallel")),
        name="out_proj",
    )(h, y1, y2, w1, w2)


def _mlp_kernel(x_ref, g_ref, w1_ref, w2_ref, o_ref, xn_ref):
    @pl.when(pl.program_id(1) == 0)
    def _():
        x = x_ref[...]
        xn_ref[...] = _rms(x, g_ref[...]).astype(BF16)
        o_ref[...] = x

    hid = jnp.maximum(_dot(xn_ref[...], w1_ref[...]), 0.0)
    o_ref[...] += _dot((hid * hid).astype(BF16), w2_ref[...])


def _mlp(h, g, w1, w2, *, tm, tf):
    t, d = h.shape
    f = w1.shape[1]
    return pl.pallas_call(
        _mlp_kernel,
        out_shape=jax.ShapeDtypeStruct((t, d), F32),
        grid=(t // tm, f // tf),
        in_specs=[
            pl.BlockSpec((tm, d), lambda i, j: (i, 0)),
            pl.BlockSpec((1, d), lambda i, j: (0, 0)),
            pl.BlockSpec((d, tf), lambda i, j: (0, j)),
            pl.BlockSpec((tf, d), lambda i, j: (j, 0)),
        ],
        out_specs=pl.BlockSpec((tm, d), lambda i, j: (i, 0)),
        scratch_shapes=[pltpu.VMEM((tm, d), BF16)],
        compiler_params=_params(("parallel", "arbitrary")),
        name="mlp",
    )(h, g, w1, w2)


def _ple_kernel(x_ref, xc_ref, g_ref, p_ref, wg_ref, wp_ref, o_ref, xn_ref):
    @pl.when(pl.program_id(1) == 0)
    def _():
        xn_ref[...] = _rms(x_ref[...], g_ref[...]).astype(BF16)

    gate = _sigmoid(_dot(xn_ref[...], wg_ref[...]))
    o_ref[...] = xc_ref[...] + gate * _dot(p_ref[...].astype(BF16), wp_ref[...])


def _ple(h, g, p, wg, wp, *, tm, tn):
    t, d = h.shape
    dp = p.shape[1]
    return pl.pallas_call(
        _ple_kernel,
        out_shape=jax.ShapeDtypeStruct((t, d), F32),
        grid=(t // tm, d // tn),
        in_specs=[
            pl.BlockSpec((tm, d), lambda i, j: (i, 0)),
            pl.BlockSpec((tm, tn), lambda i, j: (i, j)),
            pl.BlockSpec((1, d), lambda i, j: (0, 0)),
            pl.BlockSpec((tm, dp), lambda i, j: (i, 0)),
            pl.BlockSpec((d, tn), lambda i, j: (0, j)),
            pl.BlockSpec((dp, tn), lambda i, j: (0, j)),
        ],
        out_specs=pl.BlockSpec((tm, tn), lambda i, j: (i, j)),
        scratch_shapes=[pltpu.VMEM((tm, d), BF16)],
        compiler_params=_params(("parallel", "arbitrary")),
        name="ple",
    )(h, h, g, p, wg, wp)


def _final_norm_kernel(x_ref, g_ref, o_ref):
    o_ref[...] = _rms(x_ref[...], g_ref[...])


def _final_norm(h, g, *, tm):
    t, d = h.shape
    return pl.pallas_call(
        _final_norm_kernel,
        out_shape=jax.ShapeDtypeStruct((t, d), F32),
        grid=(t // tm,),
        in_specs=[pl.BlockSpec((tm, d), lambda i: (i, 0)), pl.BlockSpec((1, d), lambda i: (0, 0))],
        out_specs=pl.BlockSpec((tm, d), lambda i: (i, 0)),
        compiler_params=_params(("parallel",)),
        name="final_norm",
    )(h, g)


def _block_diag_ones(n):
    idx = jnp.arange(n) // HEAD_DIM
    return (idx[:, None] == idx[None, :]).astype(BF16)


def _pad_cols(w, n):
    return jnp.pad(w, ((0, 0), (0, n - w.shape[1])))


def _pack_w_in(w):
    n_shift = 3 * D_RWKV + N_LORA
    rkv = w[:, :3 * D_RWKV]
    lora = _pad_cols(w[:, 3 * D_RWKV:n_shift], N_LORA_PAD)
    fox = w[:, n_shift:n_shift + 3 * D_FOX]
    ff = _pad_cols(w[:, n_shift + 3 * D_FOX:], N_FF_PAD)
    return jnp.concatenate([rkv, fox, lora, ff], axis=1).astype(BF16)


def _bias_lane_tables():
    nh = D_FOX // HEAD_DIM
    h = jnp.arange(nh)
    lane = (h // 2) * LANES + (h % 2) * AUG
    sel = jnp.zeros((3, N_FF_PAD, D_FOX), F32)
    one = jnp.zeros((1, D_FOX), F32)
    for k in range(3):
        sel = sel.at[k, h, lane + k].set(1.0)
        one = one.at[0, lane + 3 + k].set(1.0)
    return sel.astype(BF16), one


def _skip_threshold(gq, gk):
    smax = 1.02 * 8.0 * jnp.max(jnp.abs(gq)) * jnp.max(jnp.abs(gk)) * LOG2E
    return (-(EXP2_UNDERFLOW + 2.0 * smax)).reshape(1).astype(F32)


def _block_starts(ct, batch, seq):
    nh = ct.shape[0]
    ends = ct.reshape(nh, batch, seq // TK, TK)[..., TK - 1]
    starts = jnp.pad(ends, ((0, 0), (0, 0), (1, 0)))
    return starts.transpose(1, 0, 2).reshape(-1)


def _lora_rows(w, start):
    return jnp.pad(w, ((start, N_LORA_PAD - start - w.shape[0]), (0, 0))).astype(BF16)


def kernel(x, p, norm_mix, w_in, mu_shift, w0, w2, a0, a2, g2, k_k, k_a, r_k, gn_w, gn_b, b_f, q_norm, k_norm, w_out, norm_mlp, w_ff1, w_ff2, norm_ple, w_ple_gate, w_ple_proj, norm_final):
    batch, seq, d = x.shape
    depth = w_in.shape[0]
    t = batch * seq
    nh_fox = D_FOX // HEAD_DIM
    tm = min(512, seq)
    sel, one = _bias_lane_tables()
    bd256 = _block_diag_ones(256)
    bd128 = _block_diag_ones(LANES)
    row = lambda a: a.reshape(1, -1)

    h = x.reshape(t, d)
    for i in range(depth):
        z = _in_proj(h, row(norm_mix[i]), _pack_w_in(w_in[i]), tm=min(1024, seq), tn=512)

        mu = mu_shift[i]
        feats = _rwkv_prep(
            z, seq, row(mu[:3 * D_RWKV]), row(jnp.pad(mu[3 * D_RWKV:], (0, N_LORA_PAD - N_LORA))),
            row(w0[i]), _lora_rows(w2[i], 0), row(a0[i]), _lora_rows(a2[i], 64),
            _lora_rows(g2[i], 128), row(k_k[i]), row(k_a[i]), bd256, tm=min(256, seq))
        y_rw = _rwkv_scan(feats[:6], feats[6], row(gn_w[i]), row(gn_b[i]), row(r_k[i]), bd256,
                          batch, seq, tc=min(512, seq))

        qt, ka, vt, ct = _fox_prep(
            z, seq, row(jnp.tile(q_norm[i], nh_fox)), row(jnp.tile(k_norm[i], nh_fox)),
            row(jnp.pad(b_f[i], (0, N_FF_PAD - nh_fox))), bd256, sel, one)
        y_fox = _fox_attention(_skip_threshold(q_norm[i], k_norm[i]), _block_starts(ct, batch, seq),
                               qt, ka, vt, ct, batch, seq)

        wo = w_out[i].astype(BF16)
        h = _out_proj(h, y_rw.reshape(t, D_RWKV), y_fox.reshape(t, D_FOX), wo[:D_RWKV], wo[D_RWKV:],
                      tm=tm, tn=512)
        h = _mlp(h, row(norm_mlp[i]), w_ff1[i].astype(BF16), w_ff2[i].astype(BF16), tm=tm, tf=512)
        h = _ple(h, row(norm_ple[i]), p[i].reshape(t, -1), w_ple_gate[i].astype(BF16),
                 w_ple_proj[i].astype(BF16), tm=tm, tn=512)
    return _final_norm(h, row(norm_final), tm=tm).reshape(batch, seq, d)
```

```python
import functools

import jax
import jax.numpy as jnp
from jax import lax
from jax.experimental import pallas as pl
from jax.experimental.pallas import tpu as pltpu

F32 = jnp.float32
BF16 = jnp.bfloat16

HEAD_DIM = 64
D_RWKV = 1024
D_FOX = 1024
N_LORA = 288
N_LORA_PAD = 384
N_FF_PAD = 128
NORM_EPS = 1e-6
GN_EPS = 64e-5
CHUNK = 64
LANES = 128
TK = 256
AUG = 8
LOG2E = 1.4426950408889634
EXP2_UNDERFLOW = 152.0
UNROLL = 4
VMEM_LIMIT = 56 * 1024 * 1024


def _dot(a, b):
    return jnp.dot(a, b, preferred_element_type=F32)


def _dot_nt(a, b):
    return lax.dot_general(a, b, (((1,), (1,)), ((), ())), preferred_element_type=F32)


def _dot_tn(a, b):
    return lax.dot_general(a, b, (((0,), (0,)), ((), ())), preferred_element_type=F32)


def _split3(x):
    h1 = x.astype(BF16)
    r1 = x - h1.astype(F32)
    h2 = r1.astype(BF16)
    h3 = (r1 - h2.astype(F32)).astype(BF16)
    return h1, h2, h3


def _split2(x):
    h1 = x.astype(BF16)
    h2 = (x - h1.astype(F32)).astype(BF16)
    return h1, h2


def _group_sum(x, bd):
    w = bd.shape[0]
    hi, lo = _split2(x)
    outs = []
    for c in range(x.shape[1] // w):
        sl = slice(c * w, (c + 1) * w)
        outs.append(_dot(hi[:, sl], bd) + _dot(lo[:, sl], bd))
    return outs[0] if len(outs) == 1 else jnp.concatenate(outs, axis=1)


def _group_sums(xs, bd):
    rows = xs[0].shape[0]
    parts = []
    for x in xs:
        parts.extend(_split2(x))
    out = _dot(jnp.concatenate(parts, axis=0), bd)
    return [out[2 * k * rows:(2 * k + 1) * rows] + out[(2 * k + 1) * rows:(2 * k + 2) * rows]
            for k in range(len(xs))]


def _cumsum_rows(tri, x):
    h1, h2, h3 = _split3(x)
    return _dot(tri, h1) + _dot(tri, h2) + _dot(tri, h3)


def _sigmoid(x):
    return 1.0 / (1.0 + jnp.exp(-x))


def _softplus(x):
    return jnp.maximum(x, 0.0) + jnp.log1p(jnp.exp(-jnp.abs(x)))


def _rms(x, g):
    ms = jnp.mean(x * x, axis=-1, keepdims=True)
    return x * lax.rsqrt(ms + NORM_EPS) * g


def _params(sem):
    return pltpu.CompilerParams(dimension_semantics=sem, vmem_limit_bytes=VMEM_LIMIT)


def _in_proj_kernel(x_ref, g_ref, w_ref, o_ref, xn_ref):
    @pl.when(pl.program_id(1) == 0)
    def _():
        xn_ref[...] = _rms(x_ref[...], g_ref[...]).astype(BF16)

    o_ref[...] = _dot(xn_ref[...], w_ref[...])


def _in_proj(h, g, w, *, tm, tn):
    t, d = h.shape
    n = w.shape[1]
    return pl.pallas_call(
        _in_proj_kernel,
        out_shape=jax.ShapeDtypeStruct((t, n), F32),
        grid=(t // tm, n // tn),
        in_specs=[
            pl.BlockSpec((tm, d), lambda i, j: (i, 0)),
            pl.BlockSpec((1, d), lambda i, j: (0, 0)),
            pl.BlockSpec((d, tn), lambda i, j: (0, j)),
        ],
        out_specs=pl.BlockSpec((tm, tn), lambda i, j: (i, j)),
        scratch_shapes=[pltpu.VMEM((tm, d), BF16)],
        compiler_params=_params(("parallel", "arbitrary")),
        name="in_proj",
    )(h, g, w)


def _rwkv_prep_kernel(zr_ref, zl_ref, pr_ref, pl_ref, mur_ref, mul_ref, w0_ref, w2_ref, a0_ref,
                      a2_ref, g2_ref, kkw_ref, kaw_ref, bd_ref,
                      r_out, ld_out, kp_out, v_out, kk_out, b_out, g_out, *, blocks_per_seq):
    first = (pl.program_id(0) % blocks_per_seq) == 0

    def shift(z, prev8, mu):
        last = jnp.where(first, 0.0, prev8[7:8, :])
        rolled = pltpu.roll(z, 1, 0)
        row = lax.broadcasted_iota(jnp.int32, z.shape, 0)
        prev = jnp.where(row == 0, last, rolled)
        return z + (prev - z) * mu

    zs = shift(zr_ref[...], pr_ref[...], mur_ref[...])
    zl = shift(zl_ref[...], pl_ref[...], mul_ref[...])
    r = zs[:, :D_RWKV]
    k = zs[:, D_RWKV:2 * D_RWKV]
    v = zs[:, 2 * D_RWKV:]

    lw = w0_ref[...] + _dot(jnp.tanh(zl).astype(BF16), w2_ref[...])
    w_log = -_softplus(-lw) - 0.5
    ld = -jnp.exp(w_log)
    a = _sigmoid(a0_ref[...] + _dot(zl.astype(BF16), a2_ref[...]))
    g = _dot(_sigmoid(zl).astype(BF16), g2_ref[...])

    kkr = k * kkw_ref[...]
    ss = _group_sum(kkr * kkr, bd_ref[...])
    kk = kkr / jnp.maximum(jnp.sqrt(ss), 1e-12)
    kp = k * (1.0 + (a - 1.0) * kaw_ref[...])

    r_out[...] = r
    ld_out[...] = ld
    kp_out[...] = kp
    v_out[...] = v
    kk_out[...] = kk
    b_out[...] = kk * a
    g_out[...] = g


def _rwkv_prep(z, seq, mu_r, mu_l, w0, w2p, a0, a2p, g2p, k_k, k_a, bd, *, tm):
    t = z.shape[0]
    c_lora = (3 * D_RWKV + 3 * D_FOX) // N_LORA_PAD
    rows8 = tm // 8
    row = lambda i: (0, 0)
    prev = lambda i: (jnp.maximum(i * rows8 - 1, 0), 0)
    out = jax.ShapeDtypeStruct((t, D_RWKV), F32)
    return pl.pallas_call(
        functools.partial(_rwkv_prep_kernel, blocks_per_seq=seq // tm),
        out_shape=[out] * 7,
        grid=(t // tm,),
        in_specs=[
            pl.BlockSpec((tm, 3 * D_RWKV), lambda i: (i, 0)),
            pl.BlockSpec((tm, N_LORA_PAD), lambda i: (i, c_lora)),
            pl.BlockSpec((8, 3 * D_RWKV), prev),
            pl.BlockSpec((8, N_LORA_PAD), lambda i: (jnp.maximum(i * rows8 - 1, 0), c_lora)),
            pl.BlockSpec((1, 3 * D_RWKV), row),
            pl.BlockSpec((1, N_LORA_PAD), row),
            pl.BlockSpec((1, D_RWKV), row),
            pl.BlockSpec((N_LORA_PAD, D_RWKV), row),
            pl.BlockSpec((1, D_RWKV), row),
            pl.BlockSpec((N_LORA_PAD, D_RWKV), row),
            pl.BlockSpec((N_LORA_PAD, D_RWKV), row),
            pl.BlockSpec((1, D_RWKV), row),
            pl.BlockSpec((1, D_RWKV), row),
            pl.BlockSpec(bd.shape, row),
        ],
        out_specs=[pl.BlockSpec((tm, D_RWKV), lambda i: (i, 0))] * 7,
        compiler_params=_params(("parallel",)),
        name="rwkv_prep",
    )(z, z, z, z, mu_r, mu_l, w0, w2p, a0, a2p, g2p, k_k, k_a, bd)


def _stack_heads(q, masks):
    zero = jnp.zeros_like(q)
    return jnp.concatenate([jnp.where(m, q, zero) for m in masks], axis=0).astype(BF16)


def _diag_blocks(full, masks):
    out = full[:HEAD_DIM]
    for g in range(1, len(masks)):
        out = jnp.where(masks[g], full[g * HEAD_DIM:(g + 1) * HEAD_DIM], out)
    return out


def _scan_kernel(r_ref, ld_ref, kp_ref, v_ref, kk_ref, b_ref, g_ref, gnw_ref, gnb_ref, rk_ref,
                 bd_ref, y_ref, s_ref, *, nchunk):
    L = CHUNK
    W = s_ref.shape[1]
    chunks = range(nchunk)

    @pl.when(pl.program_id(2) == 0)
    def _():
        s_ref[...] = jnp.zeros_like(s_ref)

    lane = lax.broadcasted_iota(jnp.int32, (1, W), 1)
    masks = [lane // HEAD_DIM == g for g in range(W // HEAD_DIM)]
    row = lax.broadcasted_iota(jnp.int32, (L, W), 0)
    col = lax.broadcasted_iota(jnp.int32, (L, W), 1) % HEAD_DIM
    strict = row > col
    incl = row >= col
    eye = (row == col).astype(F32)
    tri = (lax.broadcasted_iota(jnp.int32, (L, L), 0)
           >= lax.broadcasted_iota(jnp.int32, (L, L), 1)).astype(BF16)
    bd = bd_ref[...]
    zero = jnp.zeros((L, W), F32)

    def stack(q):
        return _stack_heads(q, masks)

    def hmm(p, qst):
        return _dot(p.astype(BF16), qst)

    sl = [slice(c * L, (c + 1) * L) for c in chunks]
    ld = [ld_ref[s, :] for s in sl]
    cum = [_cumsum_rows(tri, x) for x in ld]
    w_last = [jnp.exp(x[L - 1:L, :]) for x in cum]
    w_inv = [jnp.exp(-x) for x in cum]
    abar = [-kk_ref[sl[c], :] * jnp.exp(cum[c] - ld[c]) for c in chunks]
    rbar = [r_ref[sl[c], :] * jnp.exp(cum[c]) for c in chunks]
    btil = [b_ref[sl[c], :] * w_inv[c] for c in chunks]
    ktil = [kp_ref[sl[c], :] * w_inv[c] for c in chunks]
    bhat = [btil[c] * w_last[c] for c in chunks]
    khat = [ktil[c] * w_last[c] for c in chunks]

    a_all = [_dot_nt(jnp.concatenate([abar[c], rbar[c]], axis=0).astype(BF16),
                     jnp.concatenate([stack(btil[c]), stack(ktil[c])], axis=0))
             for c in chunks]
    n = [jnp.where(strict, a[:L, :W], zero) for a in a_all]
    a_ak = [jnp.where(strict, a[:L, W:], zero) for a in a_all]
    a_rb = [jnp.where(incl, a[L:, :W], zero) for a in a_all]
    a_rk = [jnp.where(incl, a[L:, W:], zero) for a in a_all]

    vst = [stack(v_ref[s, :]) for s in sl]
    av = [hmm(jnp.concatenate([a_ak[c], a_rk[c]], axis=0), vst[c]) for c in chunks]

    tinv = [eye + x for x in n]
    nk = [hmm(x, stack(x)) for x in n]
    for _ in range(4):
        both = [hmm(jnp.concatenate([tinv[c], nk[c]], axis=0), stack(nk[c])) for c in chunks]
        tinv = [tinv[c] + both[c][:L] for c in chunks]
        nk = [x[L:] for x in both]
    tinv = [tinv[c] + hmm(tinv[c], stack(nk[c])) for c in chunks]

    at = [hmm(tinv[c], stack(abar[c])) for c in chunks]
    u0 = [hmm(tinv[c], stack(av[c][:L])) for c in chunks]
    rt = [(rbar[c] + hmm(a_rb[c], stack(at[c]))).astype(BF16) for c in chunks]
    y0 = [hmm(a_rb[c], stack(u0[c])) + av[c][L:] for c in chunks]
    pst = [stack(_diag_blocks(_dot_tn(at[c].astype(BF16), bhat[c].astype(BF16)), masks))
           for c in chunks]
    q = [_diag_blocks(_dot_tn(jnp.concatenate([u0[c], v_ref[sl[c], :]], axis=0).astype(BF16),
                              jnp.concatenate([bhat[c], khat[c]], axis=0).astype(BF16)), masks)
         for c in chunks]

    s = s_ref[...]
    ys = []
    for c in chunks:
        ys.append(_dot_nt(rt[c], stack(s)) + y0[c])
        s = s * w_last[c] + hmm(s, pst[c]) + q[c]
    s_ref[...] = s

    sums = _group_sums(ys + [r_ref[s, :] * kp_ref[s, :] * rk_ref[...] for s in sl], bd)
    ds = [ys[c] - sums[c] * (1.0 / HEAD_DIM) for c in chunks]
    var = _group_sums([d * d for d in ds], bd)
    for c in chunks:
        yn = ds[c] * lax.rsqrt(var[c] * (1.0 / HEAD_DIM) + GN_EPS) * gnw_ref[...] + gnb_ref[...]
        bonus = sums[nchunk + c] * v_ref[sl[c], :]
        y_ref[sl[c], :] = ((yn + bonus) * g_ref[sl[c], :]).astype(y_ref.dtype)


def _rwkv_scan(feats, g, gn_w, gn_b, r_k, bd, batch, seq, *, tc):
    gw = bd.shape[0]
    tok = pl.BlockSpec((None, tc, gw), lambda b, p, c: (b, c, p))
    par = pl.BlockSpec((1, gw), lambda b, p, c: (0, p))
    feats = [f.reshape(batch, seq, D_RWKV) for f in feats]
    return pl.pallas_call(
        functools.partial(_scan_kernel, nchunk=tc // CHUNK),
        out_shape=jax.ShapeDtypeStruct((batch, seq, D_RWKV), BF16),
        grid=(batch, D_RWKV // gw, seq // tc),
        in_specs=[tok] * 7 + [par] * 3 + [pl.BlockSpec((gw, gw), lambda b, p, c: (0, 0))],
        out_specs=tok,
        scratch_shapes=[pltpu.VMEM((HEAD_DIM, gw), F32)],
        compiler_params=_params(("parallel", "parallel", "arbitrary")),
        name="rwkv_scan",
    )(*feats, g.reshape(batch, seq, D_RWKV), gn_w, gn_b, r_k, bd)


def _fox_prep_kernel(q_ref, k_ref, v_ref, ff_ref, gq_ref, gk_ref, bf_ref, bd_ref, sel_ref, one_ref,
                     qt_ref, ka_ref, vt_ref, ct_ref, carry_ref, *, blocks_per_seq):
    @pl.when((pl.program_id(0) % blocks_per_seq) == 0)
    def _():
        carry_ref[...] = jnp.zeros_like(carry_ref)

    bd = bd_ref[...]
    q = q_ref[...]
    k = k_ref[...]
    msq = _group_sum(q * q, bd) * (1.0 / HEAD_DIM)
    msk = _group_sum(k * k, bd) * (1.0 / HEAD_DIM)
    qn = q * lax.rsqrt(msq + NORM_EPS) * gq_ref[...] * (HEAD_DIM ** -0.5 * LOG2E)
    kn = k * lax.rsqrt(msk + NORM_EPS) * gk_ref[...]
    qt_ref[...] = qn.T.astype(BF16)
    vt_ref[...] = v_ref[...].T.astype(BF16)

    tm = q.shape[0]
    log_f = -_softplus(-(ff_ref[...] + bf_ref[...])) * LOG2E
    tri = (lax.broadcasted_iota(jnp.int32, (tm, tm), 0)
           >= lax.broadcasted_iota(jnp.int32, (tm, tm), 1)).astype(BF16)
    c_local = _cumsum_rows(tri, log_f)
    c = c_local + carry_ref[...]
    carry_ref[...] = c[tm - 1:tm, :]
    ct_ref[...] = c.T[:ct_ref.shape[0], :]

    n1, n2, n3 = _split3(-c_local)
    aug = _dot(n1, sel_ref[0]) + _dot(n2, sel_ref[1]) + _dot(n3, sel_ref[2]) + one_ref[...]
    for p in range(D_FOX // LANES):
        src = slice(p * LANES, (p + 1) * LANES)
        ka_ref[:, 2 * p * LANES:(2 * p + 1) * LANES] = kn[:, src].astype(BF16)
        ka_ref[:, (2 * p + 1) * LANES:(2 * p + 2) * LANES] = aug[:, src].astype(BF16)


def _fox_prep(z, seq, gq, gk, bf, bd, sel, one):
    t = z.shape[0]
    tm = TK
    nh = D_FOX // HEAD_DIM
    c_ff = (3 * D_RWKV + 3 * D_FOX + N_LORA_PAD) // N_FF_PAD
    row = lambda i: (0, 0)
    tr = jax.ShapeDtypeStruct((D_FOX, t), BF16)
    return pl.pallas_call(
        functools.partial(_fox_prep_kernel, blocks_per_seq=seq // tm),
        out_shape=[tr, jax.ShapeDtypeStruct((t, 2 * D_FOX), BF16), tr,
                   jax.ShapeDtypeStruct((nh, t), F32)],
        grid=(t // tm,),
        in_specs=[
            pl.BlockSpec((tm, D_FOX), lambda i: (i, 3)),
            pl.BlockSpec((tm, D_FOX), lambda i: (i, 4)),
            pl.BlockSpec((tm, D_FOX), lambda i: (i, 5)),
            pl.BlockSpec((tm, N_FF_PAD), lambda i: (i, c_ff)),
            pl.BlockSpec((1, D_FOX), row),
            pl.BlockSpec((1, D_FOX), row),
            pl.BlockSpec((1, N_FF_PAD), row),
            pl.BlockSpec(bd.shape, row),
            pl.BlockSpec(sel.shape, lambda i: (0, 0, 0)),
            pl.BlockSpec((1, D_FOX), row),
        ],
        out_specs=[pl.BlockSpec((D_FOX, tm), lambda i: (0, i)),
                   pl.BlockSpec((tm, 2 * D_FOX), lambda i: (i, 0)),
                   pl.BlockSpec((D_FOX, tm), lambda i: (0, i)),
                   pl.BlockSpec((nh, tm), lambda i: (0, i))],
        scratch_shapes=[pltpu.VMEM((1, N_FF_PAD), F32)],
        compiler_params=_params(("arbitrary",)),
        name="fox_prep",
    )(z, z, z, z, gq, gk, bf, bd, sel, one)


def _attn_kernel(thr_ref, cref_ref, qt_ref, ka_ref, vt_ref, ct_ref, o_ref, acc_ref, pt_ref,
                 stat_ref, *, nblk):
    def qblock(i, carry):
        _attn_qblock(i, thr_ref, cref_ref, qt_ref, ka_ref, vt_ref, ct_ref, o_ref, acc_ref, pt_ref,
                     stat_ref, nblk=nblk)
        return carry

    lax.fori_loop(0, nblk, qblock, 0)


def _attn_qblock(i, thr_ref, cref_ref, qt_ref, ka_ref, vt_ref, ct_ref, o_ref, acc_ref, pt_ref,
                 stat_ref, *, nblk):
    b = pl.program_id(0)
    p = pl.program_id(1)
    nh = ct_ref.shape[0]
    tq = TK
    qoff = pl.multiple_of(i * tq, tq)

    row_d = lax.broadcasted_iota(jnp.int32, (LANES, 1), 0)
    qt = qt_ref[:, pl.ds(qoff, tq)]
    zq = jnp.zeros_like(qt)
    qte = (jnp.where(row_d < HEAD_DIM, qt, zq), jnp.where(row_d < HEAD_DIM, zq, qt))
    rs = lax.broadcasted_iota(jnp.int32, (16, tq), 0)
    zpad = jnp.zeros((LANES - 16, tq), BF16)
    base = tuple((b * nh + 2 * p + e) * (nblk + 1) for e in range(2))
    ci = tuple(cref_ref[base[e] + i] for e in range(2))

    def rhs_operand(e):
        d = ct_ref[pl.ds(2 * p + e, 1), pl.ds(qoff, tq)] - ci[e]
        d1 = d.astype(BF16).astype(F32)
        r1 = d - d1
        d2 = r1.astype(BF16).astype(F32)
        d3 = r1 - d2
        ones3 = ((rs >= AUG * e) & (rs < AUG * e + 3)).astype(F32)
        strip = jnp.where(rs == AUG * e + 3, d1,
                          jnp.where(rs == AUG * e + 4, d2,
                                    jnp.where(rs == AUG * e + 5, d3, ones3)))
        return jnp.concatenate([qte[e], strip.astype(BF16), zpad], axis=0)

    rhs = tuple(rhs_operand(e) for e in range(2))

    def logits(off, e):
        return _dot(ka_ref[pl.ds(off, TK), :], rhs[e])

    def block_shift(j, e):
        return ci[e] - cref_ref[base[e] + j]

    def col_reduce(x, op):
        return op(op(x.reshape(8, x.shape[0] // 8, x.shape[1]), axis=0), axis=0, keepdims=True)

    def softmax_step(st, e, slot, shift):
        m = stat_ref[e, 0:1, :]
        m_new = jnp.maximum(m, col_reduce(st, jnp.max) + shift)
        alpha = jnp.exp2(m - m_new)
        pt = jnp.exp2(st - (m_new - shift))
        pt_ref[slot, e] = pt.astype(BF16)
        stat_ref[e, 0:1, :] = m_new
        stat_ref[e, 1:2, :] = alpha * stat_ref[e, 1:2, :] + col_reduce(pt, jnp.sum)
        stat_ref[e, 2 + slot:3 + slot, :] = alpha

    def pv_step(off, e, slot):
        vte = vt_ref[pl.ds(HEAD_DIM * e, HEAD_DIM), pl.ds(off, TK)]
        acc_ref[e] = (stat_ref[e, 2 + slot:3 + slot, :] * acc_ref[e]
                      + _dot(vte, pt_ref[slot, e]))

    acc_ref[...] = jnp.zeros_like(acc_ref)
    for e in range(2):
        stat_ref[e, 0:1, :] = jnp.full((1, tq), -jnp.inf, F32)
        stat_ref[e, 1:2, :] = jnp.zeros((1, tq), F32)

    thr = thr_ref[0]

    def live_blocks(e):
        def halve(_, lohi):
            lo, hi = lohi
            mid = (lo + hi) // 2
            live = block_shift(i - mid, e) >= thr
            open_ = lo < hi
            return (jnp.where(jnp.logical_and(open_, live), mid + 1, lo),
                    jnp.where(jnp.logical_and(open_, jnp.logical_not(live)), mid, hi))
        return lax.fori_loop(0, nblk.bit_length(), halve, (jnp.int32(0), i))[0]

    nlive = jnp.maximum(live_blocks(0), live_blocks(1))

    def step(first, nb, diagonal):
        offs = [pl.multiple_of((first - t) * TK, TK) for t in range(nb)]
        sts = [tuple(logits(offs[t], e) for e in range(2)) for t in range(nb)]
        if diagonal:
            tt = lax.broadcasted_iota(jnp.int32, (TK, tq), 0)
            rr = lax.broadcasted_iota(jnp.int32, (TK, tq), 1)
            sts[0] = tuple(jnp.where(tt <= rr, st, -jnp.inf) for st in sts[0])
        else:
            for e in range(2):
                pv_step(pl.multiple_of(offs[0] + TK, TK), e, 0)
        for t in range(nb):
            slot = (nb - 1 - t) % 2
            for e in range(2):
                softmax_step(sts[t][e], e, slot, block_shift(first - t, e))
            if t < nb - 1:
                for e in range(2):
                    pv_step(offs[t], e, slot)

    rem = (nlive + 1) % UNROLL
    pl.when(rem % 2 == 1)(lambda: step(i, 1, True))
    pl.when(rem == 2)(lambda: step(i, 2, True))
    pl.when(rem == 3)(lambda: step(i - 1, 2, False))
    pl.when(rem == 0)(lambda: step(i, UNROLL, True))
    head = jnp.where(rem == 0, UNROLL, rem)

    def body(k, carry):
        step(i - head - UNROLL * k, UNROLL, False)
        return carry

    lax.fori_loop(0, (nlive + 1 - head) // UNROLL, body, 0)
    last = pl.multiple_of((i - nlive) * TK, TK)
    for e in range(2):
        pv_step(last, e, 0)
    ot = jnp.concatenate([acc_ref[e] / stat_ref[e, 1:2, :] for e in range(2)], axis=0)
    o_ref[pl.ds(qoff, tq), :] = ot.T.astype(o_ref.dtype)


def _fox_attention(thr, cref, qt, ka, vt, ct, batch, seq):
    tq = TK
    npair = D_FOX // LANES
    nh = D_FOX // HEAD_DIM
    smem = pl.BlockSpec(memory_space=pltpu.SMEM)
    return pl.pallas_call(
        functools.partial(_attn_kernel, nblk=seq // TK),
        out_shape=jax.ShapeDtypeStruct((batch, seq, D_FOX), BF16),
        grid=(batch, npair),
        in_specs=[
            smem, smem,
            pl.BlockSpec((LANES, seq), lambda b, p: (p, b)),
            pl.BlockSpec((seq, 2 * LANES), lambda b, p: (b, p)),
            pl.BlockSpec((LANES, seq), lambda b, p: (p, b)),
            pl.BlockSpec((nh, seq), lambda b, p: (0, b)),
        ],
        out_specs=pl.BlockSpec((None, seq, LANES), lambda b, p: (b, 0, p)),
        scratch_shapes=[pltpu.VMEM((2, HEAD_DIM, tq), F32), pltpu.VMEM((2, 2, TK, tq), BF16),
                        pltpu.VMEM((2, 8, tq), F32)],
        compiler_params=_params(("parallel", "parallel")),
        name="fox_attn",
    )(thr, cref, qt, ka, vt, ct)


def _out_proj_kernel(h_ref, y1_ref, y2_ref, w1_ref, w2_ref, o_ref):
    o_ref[...] = h_ref[...] + _dot(y1_ref[...], w1_ref[...]) + _dot(y2_ref[...], w2_ref[...])


def _resident(shape):
    return pl.BlockSpec(shape, lambda i: (0,) * len(shape), pipeline_mode=pl.Buffered(1))


def _out_proj(h, y1, y2, w1, w2, *, tm):
    t, d = h.shape
    return pl.pallas_call(
        _out_proj_kernel,
        out_shape=jax.ShapeDtypeStruct((t, d), F32),
        grid=(t // tm,),
        in_specs=[
            pl.BlockSpec((tm, d), lambda i: (i, 0)),
            pl.BlockSpec((tm, y1.shape[1]), lambda i: (i, 0)),
            pl.BlockSpec((tm, y2.shape[1]), lambda i: (i, 0)),
            _resident(w1.shape),
            _resident(w2.shape),
        ],
        out_specs=pl.BlockSpec((tm, d), lambda i: (i, 0)),
        compiler_params=_params(("parallel",)),
        name="out_proj",
    )(h, y1, y2, w1, w2)


def _mlp_kernel(x_ref, g_ref, w1_ref, w2_ref, o_ref, xn_ref):
    @pl.when(pl.program_id(1) == 0)
    def _():
        x = x_ref[...]
        xn_ref[...] = _rms(x, g_ref[...]).astype(BF16)
        o_ref[...] = x

    hid = jnp.maximum(_dot(xn_ref[...], w1_ref[...]), 0.0)
    o_ref[...] += _dot((hid * hid).astype(BF16), w2_ref[...])


def _mlp(h, g, w1, w2, *, tm, tf):
    t, d = h.shape
    f = w1.shape[1]
    return pl.pallas_call(
        _mlp_kernel,
        out_shape=jax.ShapeDtypeStruct((t, d), F32),
        grid=(t // tm, f // tf),
        in_specs=[
            pl.BlockSpec((tm, d), lambda i, j: (i, 0)),
            pl.BlockSpec((1, d), lambda i, j: (0, 0)),
            pl.BlockSpec((d, tf), lambda i, j: (0, j)),
            pl.BlockSpec((tf, d), lambda i, j: (j, 0)),
        ],
        out_specs=pl.BlockSpec((tm, d), lambda i, j: (i, 0)),
        scratch_shapes=[pltpu.VMEM((tm, d), BF16)],
        compiler_params=_params(("parallel", "arbitrary")),
        name="mlp",
    )(h, g, w1, w2)


def _ple_kernel(x_ref, g_ref, p_ref, wg_ref, wp_ref, gf_ref, o_ref, *, final):
    x = x_ref[...]
    gate = _sigmoid(_dot(_rms(x, g_ref[...]).astype(BF16), wg_ref[...]))
    h = x + gate * _dot(p_ref[...].astype(BF16), wp_ref[...])
    o_ref[...] = _rms(h, gf_ref[...]) if final else h


def _ple(h, g, p, wg, wp, g_final, *, tm, final):
    t, d = h.shape
    dp = p.shape[1]
    return pl.pallas_call(
        functools.partial(_ple_kernel, final=final),
        out_shape=jax.ShapeDtypeStruct((t, d), F32),
        grid=(t // tm,),
        in_specs=[
            pl.BlockSpec((tm, d), lambda i: (i, 0)),
            _resident((1, d)),
            pl.BlockSpec((tm, dp), lambda i: (i, 0)),
            _resident(wg.shape),
            _resident(wp.shape),
            _resident((1, d)),
        ],
        out_specs=pl.BlockSpec((tm, d), lambda i: (i, 0)),
        compiler_params=_params(("parallel",)),
        name="ple",
    )(h, g, p, wg, wp, g_final)


def _block_diag_ones(n):
    idx = jnp.arange(n) // HEAD_DIM
    return (idx[:, None] == idx[None, :]).astype(BF16)


def _pad_cols(w, n):
    return jnp.pad(w, ((0, 0), (0, n - w.shape[1])))


def _take_heads(w, order, axis):
    shape = w.shape
    split = shape[:axis] + (order.shape[0], shape[axis] // order.shape[0]) + shape[axis + 1:]
    return jnp.take(w.reshape(split), order, axis=axis).reshape(shape)


def _pack_w_in(w, order):
    n_shift = 3 * D_RWKV + N_LORA
    rkv = w[:, :3 * D_RWKV]
    lora = _pad_cols(w[:, 3 * D_RWKV:n_shift], N_LORA_PAD)
    fox = [_take_heads(w[:, n_shift + k * D_FOX:n_shift + (k + 1) * D_FOX], order, 1) for k in range(3)]
    ff = _pad_cols(jnp.take(w[:, n_shift + 3 * D_FOX:], order, axis=1), N_FF_PAD)
    return jnp.concatenate([rkv] + fox + [lora, ff], axis=1).astype(BF16)


def _bias_lane_tables():
    nh = D_FOX // HEAD_DIM
    h = jnp.arange(nh)
    lane = (h // 2) * LANES + (h % 2) * AUG
    sel = jnp.zeros((3, N_FF_PAD, D_FOX), F32)
    one = jnp.zeros((1, D_FOX), F32)
    for k in range(3):
        sel = sel.at[k, h, lane + k].set(1.0)
        one = one.at[0, lane + 3 + k].set(1.0)
    return sel.astype(BF16), one


def _skip_threshold(gq, gk):
    smax = 1.02 * 8.0 * jnp.max(jnp.abs(gq)) * jnp.max(jnp.abs(gk)) * LOG2E
    return (-(EXP2_UNDERFLOW + 2.0 * smax)).reshape(1).astype(F32)


def _block_starts(ct, batch, seq):
    nh = ct.shape[0]
    ends = ct.reshape(nh, batch, seq // TK, TK)[..., TK - 1]
    starts = jnp.pad(ends, ((0, 0), (0, 0), (1, 0)))
    return starts.transpose(1, 0, 2).reshape(-1)


def _lora_rows(w, start):
    return jnp.pad(w, ((start, N_LORA_PAD - start - w.shape[0]), (0, 0))).astype(BF16)


def kernel(x, p, norm_mix, w_in, mu_shift, w0, w2, a0, a2, g2, k_k, k_a, r_k, gn_w, gn_b, b_f, q_norm, k_norm, w_out, norm_mlp, w_ff1, w_ff2, norm_ple, w_ple_gate, w_ple_proj, norm_final):
    batch, seq, d = x.shape
    depth = w_in.shape[0]
    t = batch * seq
    nh_fox = D_FOX // HEAD_DIM
    tm = min(512, seq)
    sel, one = _bias_lane_tables()
    bd256 = _block_diag_ones(256)
    bd128 = _block_diag_ones(LANES)
    row = lambda a: a.reshape(1, -1)

    h = x.reshape(t, d)
    for i in range(depth):
        order = jnp.argsort(b_f[i])
        z = _in_proj(h, row(norm_mix[i]), _pack_w_in(w_in[i], order), tm=min(1024, seq), tn=512)

        mu = mu_shift[i]
        feats = _rwkv_prep(
            z, seq, row(mu[:3 * D_RWKV]), row(jnp.pad(mu[3 * D_RWKV:], (0, N_LORA_PAD - N_LORA))),
            row(w0[i]), _lora_rows(w2[i], 0), row(a0[i]), _lora_rows(a2[i], 64),
            _lora_rows(g2[i], 128), row(k_k[i]), row(k_a[i]), bd256, tm=min(256, seq))
        y_rw = _rwkv_scan(feats[:6], feats[6], row(gn_w[i]), row(gn_b[i]), row(r_k[i]), bd256,
                          batch, seq, tc=min(512, seq))

        qt, ka, vt, ct = _fox_prep(
            z, seq, row(jnp.tile(q_norm[i], nh_fox)), row(jnp.tile(k_norm[i], nh_fox)),
            row(jnp.pad(jnp.take(b_f[i], order), (0, N_FF_PAD - nh_fox))), bd256, sel, one)
        y_fox = _fox_attention(_skip_threshold(q_norm[i], k_norm[i]), _block_starts(ct, batch, seq),
                               qt, ka, vt, ct, batch, seq)

        wo = w_out[i].astype(BF16)
        h = _out_proj(h, y_rw.reshape(t, D_RWKV), y_fox.reshape(t, D_FOX), wo[:D_RWKV],
                      _take_heads(wo[D_RWKV:], order, 0), tm=tm)
        h = _mlp(h, row(norm_mlp[i]), w_ff1[i].astype(BF16), w_ff2[i].astype(BF16), tm=tm, tf=512)
        h = _ple(h, row(norm_ple[i]), p[i].reshape(t, -1), w_ple_gate[i].astype(BF16),
                 w_ple_proj[i].astype(BF16), row(norm_final), tm=tm, final=i == depth - 1)
    return h.reshape(batch, seq, d)
```

```python
import functools

import jax
import jax.numpy as jnp
from jax import lax
from jax.experimental import pallas as pl
from jax.experimental.pallas import tpu as pltpu

F32 = jnp.float32
BF16 = jnp.bfloat16

HEAD_DIM = 64
D_RWKV = 1024
D_FOX = 1024
N_LORA = 288
N_LORA_PAD = 384
N_FF_PAD = 128
NORM_EPS = 1e-6
GN_EPS = 64e-5
CHUNK = 64
LANES = 128
TK = 256
AUG = 8
LOG2E = 1.4426950408889634
EXP2_UNDERFLOW = 152.0
UNROLL = 8
LOOKAHEAD = 2
VMEM_LIMIT = 56 * 1024 * 1024


def _dot(a, b):
    return jnp.dot(a, b, preferred_element_type=F32)


def _dot_nt(a, b):
    return lax.dot_general(a, b, (((1,), (1,)), ((), ())), preferred_element_type=F32)


def _dot_tn(a, b):
    return lax.dot_general(a, b, (((0,), (0,)), ((), ())), preferred_element_type=F32)


def _split3(x):
    h1 = x.astype(BF16)
    r1 = x - h1.astype(F32)
    h2 = r1.astype(BF16)
    h3 = (r1 - h2.astype(F32)).astype(BF16)
    return h1, h2, h3


def _split2(x):
    h1 = x.astype(BF16)
    h2 = (x - h1.astype(F32)).astype(BF16)
    return h1, h2


def _group_sum(x, bd):
    w = bd.shape[0]
    hi, lo = _split2(x)
    outs = []
    for c in range(x.shape[1] // w):
        sl = slice(c * w, (c + 1) * w)
        outs.append(_dot(hi[:, sl], bd) + _dot(lo[:, sl], bd))
    return outs[0] if len(outs) == 1 else jnp.concatenate(outs, axis=1)


def _group_sums(xs, bd):
    rows = xs[0].shape[0]
    parts = []
    for x in xs:
        parts.extend(_split2(x))
    out = _dot(jnp.concatenate(parts, axis=0), bd)
    return [out[2 * k * rows:(2 * k + 1) * rows] + out[(2 * k + 1) * rows:(2 * k + 2) * rows]
            for k in range(len(xs))]


def _cumsum_rows(tri, x):
    h1, h2, h3 = _split3(x)
    return _dot(tri, h1) + _dot(tri, h2) + _dot(tri, h3)


def _sigmoid(x):
    return 1.0 / (1.0 + jnp.exp(-x))


def _softplus(x):
    return jnp.maximum(x, 0.0) + jnp.log1p(jnp.exp(-jnp.abs(x)))


def _rms(x, g):
    ms = jnp.mean(x * x, axis=-1, keepdims=True)
    return x * lax.rsqrt(ms + NORM_EPS) * g


def _params(sem):
    return pltpu.CompilerParams(dimension_semantics=sem, vmem_limit_bytes=VMEM_LIMIT)


def _in_proj_kernel(x_ref, g_ref, w_ref, o_ref, xn_ref):
    @pl.when(pl.program_id(1) == 0)
    def _():
        xn_ref[...] = _rms(x_ref[...], g_ref[...]).astype(BF16)

    o_ref[...] = _dot(xn_ref[...], w_ref[...])


def _in_proj(h, g, w, *, tm, tn):
    t, d = h.shape
    n = w.shape[1]
    return pl.pallas_call(
        _in_proj_kernel,
        out_shape=jax.ShapeDtypeStruct((t, n), F32),
        grid=(t // tm, n // tn),
        in_specs=[
            pl.BlockSpec((tm, d), lambda i, j: (i, 0)),
            pl.BlockSpec((1, d), lambda i, j: (0, 0)),
            pl.BlockSpec((d, tn), lambda i, j: (0, j)),
        ],
        out_specs=pl.BlockSpec((tm, tn), lambda i, j: (i, j)),
        scratch_shapes=[pltpu.VMEM((tm, d), BF16)],
        compiler_params=_params(("parallel", "arbitrary")),
        name="in_proj",
    )(h, g, w)


def _rwkv_prep_kernel(zr_ref, zl_ref, pr_ref, pl_ref, mur_ref, mul_ref, w0_ref, w2_ref, a0_ref,
                      a2_ref, g2_ref, kkw_ref, kaw_ref, bd_ref,
                      r_out, ld_out, kp_out, v_out, kk_out, b_out, g_out, *, blocks_per_seq):
    first = (pl.program_id(0) % blocks_per_seq) == 0

    def shift(z, prev8, mu):
        last = jnp.where(first, 0.0, prev8[7:8, :])
        rolled = pltpu.roll(z, 1, 0)
        row = lax.broadcasted_iota(jnp.int32, z.shape, 0)
        prev = jnp.where(row == 0, last, rolled)
        return z + (prev - z) * mu

    zs = shift(zr_ref[...], pr_ref[...], mur_ref[...])
    zl = shift(zl_ref[...], pl_ref[...], mul_ref[...])
    r = zs[:, :D_RWKV]
    k = zs[:, D_RWKV:2 * D_RWKV]
    v = zs[:, 2 * D_RWKV:]

    lw = w0_ref[...] + _dot(jnp.tanh(zl).astype(BF16), w2_ref[...])
    w_log2 = ((jnp.minimum(lw, 0.0) - 0.5) * LOG2E
              - jnp.log2(1.0 + jnp.exp2(jnp.abs(lw) * -LOG2E)))
    ld = -jnp.exp2(w_log2)
    a = _sigmoid(a0_ref[...] + _dot(zl.astype(BF16), a2_ref[...]))
    g = _dot(_sigmoid(zl).astype(BF16), g2_ref[...])

    kkr = k * kkw_ref[...]
    ss = _group_sum(kkr * kkr, bd_ref[...])
    kk = kkr * lax.rsqrt(jnp.maximum(ss, 1e-24))
    kp = k * (1.0 + (a - 1.0) * kaw_ref[...])

    r_out[...] = r
    ld_out[...] = ld
    kp_out[...] = kp
    v_out[...] = v
    kk_out[...] = kk
    b_out[...] = kk * a
    g_out[...] = g


def _rwkv_prep(z, seq, mu_r, mu_l, w0, w2p, a0, a2p, g2p, k_k, k_a, bd, *, tm):
    t = z.shape[0]
    c_lora = (3 * D_RWKV + 3 * D_FOX) // N_LORA_PAD
    rows8 = tm // 8
    row = lambda i: (0, 0)
    prev = lambda i: (jnp.maximum(i * rows8 - 1, 0), 0)
    out = jax.ShapeDtypeStruct((t, D_RWKV), F32)
    return pl.pallas_call(
        functools.partial(_rwkv_prep_kernel, blocks_per_seq=seq // tm),
        out_shape=[out] * 7,
        grid=(t // tm,),
        in_specs=[
            pl.BlockSpec((tm, 3 * D_RWKV), lambda i: (i, 0)),
            pl.BlockSpec((tm, N_LORA_PAD), lambda i: (i, c_lora)),
            pl.BlockSpec((8, 3 * D_RWKV), prev),
            pl.BlockSpec((8, N_LORA_PAD), lambda i: (jnp.maximum(i * rows8 - 1, 0), c_lora)),
            pl.BlockSpec((1, 3 * D_RWKV), row),
            pl.BlockSpec((1, N_LORA_PAD), row),
            pl.BlockSpec((1, D_RWKV), row),
            pl.BlockSpec((N_LORA_PAD, D_RWKV), row),
            pl.BlockSpec((1, D_RWKV), row),
            pl.BlockSpec((N_LORA_PAD, D_RWKV), row),
            pl.BlockSpec((N_LORA_PAD, D_RWKV), row),
            pl.BlockSpec((1, D_RWKV), row),
            pl.BlockSpec((1, D_RWKV), row),
            pl.BlockSpec(bd.shape, row),
        ],
        out_specs=[pl.BlockSpec((tm, D_RWKV), lambda i: (i, 0))] * 7,
        compiler_params=_params(("parallel",)),
        name="rwkv_prep",
    )(z, z, z, z, mu_r, mu_l, w0, w2p, a0, a2p, g2p, k_k, k_a, bd)


def _stack_heads(q, masks):
    zero = jnp.zeros_like(q)
    return jnp.concatenate([jnp.where(m, q, zero) for m in masks], axis=0).astype(BF16)


def _diag_blocks(full, masks):
    out = full[:HEAD_DIM]
    for g in range(1, len(masks)):
        out = jnp.where(masks[g], full[g * HEAD_DIM:(g + 1) * HEAD_DIM], out)
    return out


def _scan_kernel(r_ref, ld_ref, kp_ref, v_ref, kk_ref, b_ref, g_ref, gnw_ref, gnb_ref, rk_ref,
                 bd_ref, y_ref, s_ref, *, nchunk):
    L = CHUNK
    W = s_ref.shape[1]
    chunks = range(nchunk)

    @pl.when(pl.program_id(2) == 0)
    def _():
        s_ref[...] = jnp.zeros_like(s_ref)

    lane = lax.broadcasted_iota(jnp.int32, (1, W), 1)
    masks = [lane // HEAD_DIM == g for g in range(W // HEAD_DIM)]
    row = lax.broadcasted_iota(jnp.int32, (L, W), 0)
    col = lax.broadcasted_iota(jnp.int32, (L, W), 1) % HEAD_DIM
    strict = row > col
    incl = row >= col
    eye = (row == col).astype(F32)
    tri = (lax.broadcasted_iota(jnp.int32, (L, L), 0)
           >= lax.broadcasted_iota(jnp.int32, (L, L), 1)).astype(BF16)
    bd = bd_ref[...]
    zero = jnp.zeros((L, W), F32)

    def stack(q):
        return _stack_heads(q, masks)

    def hmm(p, qst):
        return _dot(p.astype(BF16), qst)

    sl = [slice(c * L, (c + 1) * L) for c in chunks]
    ld = [ld_ref[s, :] for s in sl]
    cum = [_cumsum_rows(tri, x) for x in ld]
    w_last = [jnp.exp(x[L - 1:L, :]) for x in cum]
    w_inv = [jnp.exp(-x) for x in cum]
    abar = [-kk_ref[sl[c], :] * jnp.exp(cum[c] - ld[c]) for c in chunks]
    rbar = [r_ref[sl[c], :] * jnp.exp(cum[c]) for c in chunks]
    btil = [b_ref[sl[c], :] * w_inv[c] for c in chunks]
    ktil = [kp_ref[sl[c], :] * w_inv[c] for c in chunks]
    bhat = [btil[c] * w_last[c] for c in chunks]
    khat = [ktil[c] * w_last[c] for c in chunks]

    a_all = [_dot_nt(jnp.concatenate([abar[c], rbar[c]], axis=0).astype(BF16),
                     jnp.concatenate([stack(btil[c]), stack(ktil[c])], axis=0))
             for c in chunks]
    n = [jnp.where(strict, a[:L, :W], zero) for a in a_all]
    a_ak = [jnp.where(strict, a[:L, W:], zero) for a in a_all]
    a_rb = [jnp.where(incl, a[L:, :W], zero) for a in a_all]
    a_rk = [jnp.where(incl, a[L:, W:], zero) for a in a_all]

    vst = [stack(v_ref[s, :]) for s in sl]
    av = [hmm(jnp.concatenate([a_ak[c], a_rk[c]], axis=0), vst[c]) for c in chunks]

    tinv = [eye + x for x in n]
    nk = [hmm(x, stack(x)) for x in n]
    for _ in range(4):
        both = [hmm(jnp.concatenate([tinv[c], nk[c]], axis=0), stack(nk[c])) for c in chunks]
        tinv = [tinv[c] + both[c][:L] for c in chunks]
        nk = [x[L:] for x in both]
    tinv = [tinv[c] + hmm(tinv[c], stack(nk[c])) for c in chunks]

    at = [hmm(tinv[c], stack(abar[c])) for c in chunks]
    u0 = [hmm(tinv[c], stack(av[c][:L])) for c in chunks]
    rt = [(rbar[c] + hmm(a_rb[c], stack(at[c]))).astype(BF16) for c in chunks]
    y0 = [hmm(a_rb[c], stack(u0[c])) + av[c][L:] for c in chunks]
    pst = [stack(_diag_blocks(_dot_tn(at[c].astype(BF16), bhat[c].astype(BF16)), masks))
           for c in chunks]
    q = [_diag_blocks(_dot_tn(jnp.concatenate([u0[c], v_ref[sl[c], :]], axis=0).astype(BF16),
                              jnp.concatenate([bhat[c], khat[c]], axis=0).astype(BF16)), masks)
         for c in chunks]

    s = s_ref[...]
    ys = []
    for c in chunks:
        ys.append(_dot_nt(rt[c], stack(s)) + y0[c])
        s = s * w_last[c] + hmm(s, pst[c]) + q[c]
    s_ref[...] = s

    sums = _group_sums(ys + [r_ref[s, :] * kp_ref[s, :] * rk_ref[...] for s in sl], bd)
    ds = [ys[c] - sums[c] * (1.0 / HEAD_DIM) for c in chunks]
    var = _group_sums([d * d for d in ds], bd)
    for c in chunks:
        yn = ds[c] * lax.rsqrt(var[c] * (1.0 / HEAD_DIM) + GN_EPS) * gnw_ref[...] + gnb_ref[...]
        bonus = sums[nchunk + c] * v_ref[sl[c], :]
        y_ref[sl[c], :] = ((yn + bonus) * g_ref[sl[c], :]).astype(y_ref.dtype)


def _rwkv_scan(feats, g, gn_w, gn_b, r_k, bd, batch, seq, *, tc):
    gw = bd.shape[0]
    tok = pl.BlockSpec((None, tc, gw), lambda b, p, c: (b, c, p))
    par = pl.BlockSpec((1, gw), lambda b, p, c: (0, p))
    feats = [f.reshape(batch, seq, D_RWKV) for f in feats]
    return pl.pallas_call(
        functools.partial(_scan_kernel, nchunk=tc // CHUNK),
        out_shape=jax.ShapeDtypeStruct((batch, seq, D_RWKV), BF16),
        grid=(batch, D_RWKV // gw, seq // tc),
        in_specs=[tok] * 7 + [par] * 3 + [pl.BlockSpec((gw, gw), lambda b, p, c: (0, 0))],
        out_specs=tok,
        scratch_shapes=[pltpu.VMEM((HEAD_DIM, gw), F32)],
        compiler_params=_params(("parallel", "parallel", "arbitrary")),
        name="rwkv_scan",
    )(*feats, g.reshape(batch, seq, D_RWKV), gn_w, gn_b, r_k, bd)


def _fox_prep_kernel(q_ref, k_ref, v_ref, ff_ref, gq_ref, gk_ref, bf_ref, bd_ref, sel_ref, one_ref,
                     qt_ref, ka_ref, vt_ref, ct_ref, carry_ref, *, blocks_per_seq):
    @pl.when((pl.program_id(0) % blocks_per_seq) == 0)
    def _():
        carry_ref[...] = jnp.zeros_like(carry_ref)

    bd = bd_ref[...]
    q = q_ref[...]
    k = k_ref[...]
    msq = _group_sum(q * q, bd) * (1.0 / HEAD_DIM)
    msk = _group_sum(k * k, bd) * (1.0 / HEAD_DIM)
    qn = q * lax.rsqrt(msq + NORM_EPS) * gq_ref[...] * (HEAD_DIM ** -0.5 * LOG2E)
    kn = k * lax.rsqrt(msk + NORM_EPS) * gk_ref[...]
    qt_ref[...] = qn.T.astype(BF16)
    vt_ref[...] = v_ref[...].T.astype(BF16)

    tm = q.shape[0]
    log_f = -_softplus(-(ff_ref[...] + bf_ref[...])) * LOG2E
    tri = (lax.broadcasted_iota(jnp.int32, (tm, tm), 0)
           >= lax.broadcasted_iota(jnp.int32, (tm, tm), 1)).astype(BF16)
    c_local = _cumsum_rows(tri, log_f)
    c = c_local + carry_ref[...]
    carry_ref[...] = c[tm - 1:tm, :]
    ct_ref[...] = c.T[:ct_ref.shape[0], :]

    n1, n2, n3 = _split3(-c_local)
    aug = _dot(n1, sel_ref[0]) + _dot(n2, sel_ref[1]) + _dot(n3, sel_ref[2]) + one_ref[...]
    for p in range(D_FOX // LANES):
        src = slice(p * LANES, (p + 1) * LANES)
        ka_ref[:, 2 * p * LANES:(2 * p + 1) * LANES] = kn[:, src].astype(BF16)
        ka_ref[:, (2 * p + 1) * LANES:(2 * p + 2) * LANES] = aug[:, src].astype(BF16)


def _fox_prep(z, seq, gq, gk, bf, bd, sel, one):
    t = z.shape[0]
    tm = TK
    nh = D_FOX // HEAD_DIM
    c_ff = (3 * D_RWKV + 3 * D_FOX + N_LORA_PAD) // N_FF_PAD
    row = lambda i: (0, 0)
    tr = jax.ShapeDtypeStruct((D_FOX, t), BF16)
    return pl.pallas_call(
        functools.partial(_fox_prep_kernel, blocks_per_seq=seq // tm),
        out_shape=[tr, jax.ShapeDtypeStruct((t, 2 * D_FOX), BF16), tr,
                   jax.ShapeDtypeStruct((nh, t), F32)],
        grid=(t // tm,),
        in_specs=[
            pl.BlockSpec((tm, D_FOX), lambda i: (i, 3)),
            pl.BlockSpec((tm, D_FOX), lambda i: (i, 4)),
            pl.BlockSpec((tm, D_FOX), lambda i: (i, 5)),
            pl.BlockSpec((tm, N_FF_PAD), lambda i: (i, c_ff)),
            pl.BlockSpec((1, D_FOX), row),
            pl.BlockSpec((1, D_FOX), row),
            pl.BlockSpec((1, N_FF_PAD), row),
            pl.BlockSpec(bd.shape, row),
            pl.BlockSpec(sel.shape, lambda i: (0, 0, 0)),
            pl.BlockSpec((1, D_FOX), row),
        ],
        out_specs=[pl.BlockSpec((D_FOX, tm), lambda i: (0, i)),
                   pl.BlockSpec((tm, 2 * D_FOX), lambda i: (i, 0)),
                   pl.BlockSpec((D_FOX, tm), lambda i: (0, i)),
                   pl.BlockSpec((nh, tm), lambda i: (0, i))],
        scratch_shapes=[pltpu.VMEM((1, N_FF_PAD), F32)],
        compiler_params=_params(("arbitrary",)),
        name="fox_prep",
    )(z, z, z, z, gq, gk, bf, bd, sel, one)


def _attn_kernel(thr_ref, cref_ref, qt_ref, ka_ref, vt_ref, ct_ref, o_ref, acc_ref, pt_ref,
                 stat_ref, *, nblk):
    def qblock(i, nlive):
        return _attn_qblock(i, nlive, thr_ref, cref_ref, qt_ref, ka_ref, vt_ref, ct_ref, o_ref,
                            acc_ref, pt_ref, stat_ref, nblk=nblk)

    lax.fori_loop(0, nblk, qblock, jnp.int32(0))


def _live_blocks(i, thr, cref_ref, base, nblk):
    ci = [cref_ref[b + i] for b in base]
    lo = [jnp.int32(0)] * len(base)
    hi = [i] * len(base)
    for _ in range(nblk.bit_length()):
        for e, b in enumerate(base):
            mid = (lo[e] + hi[e]) // 2
            live = ci[e] - cref_ref[b + i - mid] >= thr
            is_open = lo[e] < hi[e]
            lo[e], hi[e] = (jnp.where(jnp.logical_and(is_open, live), mid + 1, lo[e]),
                            jnp.where(jnp.logical_and(is_open, jnp.logical_not(live)), mid, hi[e]))
    return functools.reduce(jnp.maximum, lo)


def _attn_qblock(i, nlive, thr_ref, cref_ref, qt_ref, ka_ref, vt_ref, ct_ref, o_ref, acc_ref,
                 pt_ref, stat_ref, *, nblk):
    b = pl.program_id(0)
    p = pl.program_id(1)
    nh = ct_ref.shape[0]
    tq = TK
    qoff = pl.multiple_of(i * tq, tq)

    row_d = lax.broadcasted_iota(jnp.int32, (LANES, 1), 0)
    qt = qt_ref[:, pl.ds(qoff, tq)]
    zq = jnp.zeros_like(qt)
    qte = (jnp.where(row_d < HEAD_DIM, qt, zq), jnp.where(row_d < HEAD_DIM, zq, qt))
    rs = lax.broadcasted_iota(jnp.int32, (16, tq), 0)
    zpad = jnp.zeros((LANES - 16, tq), BF16)
    base = tuple((b * nh + 2 * p + e) * (nblk + 1) for e in range(2))
    ci = tuple(cref_ref[base[e] + i] for e in range(2))

    def rhs_operand(e):
        d = ct_ref[pl.ds(2 * p + e, 1), pl.ds(qoff, tq)] - ci[e]
        d1 = d.astype(BF16).astype(F32)
        r1 = d - d1
        d2 = r1.astype(BF16).astype(F32)
        d3 = r1 - d2
        ones3 = ((rs >= AUG * e) & (rs < AUG * e + 3)).astype(F32)
        strip = jnp.where(rs == AUG * e + 3, d1,
                          jnp.where(rs == AUG * e + 4, d2,
                                    jnp.where(rs == AUG * e + 5, d3, ones3)))
        return jnp.concatenate([qte[e], strip.astype(BF16), zpad], axis=0)

    rhs = tuple(rhs_operand(e) for e in range(2))

    def logits(off, e):
        return _dot(ka_ref[pl.ds(off, TK), :], rhs[e])

    def block_shift(j, e):
        return ci[e] - cref_ref[base[e] + j]

    def col_reduce(x, op):
        return op(op(x.reshape(8, x.shape[0] // 8, x.shape[1]), axis=0), axis=0, keepdims=True)

    def softmax_step(st, e, slot, shift):
        m = stat_ref[e, 0:1, :]
        m_new = jnp.maximum(m, col_reduce(st, jnp.max) + shift)
        alpha = jnp.exp2(m - m_new)
        pt = jnp.exp2(st - (m_new - shift))
        pt_ref[slot, e] = pt.astype(BF16)
        stat_ref[e, 0:1, :] = m_new
        stat_ref[e, 1:2, :] = alpha * stat_ref[e, 1:2, :] + col_reduce(pt, jnp.sum)
        stat_ref[e, 2 + slot:3 + slot, :] = alpha

    def pv_step(off, e, slot):
        vte = vt_ref[pl.ds(HEAD_DIM * e, HEAD_DIM), pl.ds(off, TK)]
        acc_ref[e] = (stat_ref[e, 2 + slot:3 + slot, :] * acc_ref[e]
                      + _dot(vte, pt_ref[slot, e]))

    acc_ref[...] = jnp.zeros_like(acc_ref)
    for e in range(2):
        stat_ref[e, 0:1, :] = jnp.full((1, tq), -jnp.inf, F32)
        stat_ref[e, 1:2, :] = jnp.zeros((1, tq), F32)

    thr = thr_ref[0]

    nlive_next = _live_blocks(i + 1, thr, cref_ref, base, nblk)

    def step(first, nb, diagonal):
        offs = [pl.multiple_of((first - t) * TK, TK) for t in range(nb)]
        sts = [None] * nb
        for t in range(min(LOOKAHEAD, nb)):
            sts[t] = tuple(logits(offs[t], e) for e in range(2))
        if diagonal:
            tt = lax.broadcasted_iota(jnp.int32, (TK, tq), 0)
            rr = lax.broadcasted_iota(jnp.int32, (TK, tq), 1)
            sts[0] = tuple(jnp.where(tt <= rr, st, -jnp.inf) for st in sts[0])
        else:
            for e in range(2):
                pv_step(pl.multiple_of(offs[0] + TK, TK), e, 0)
        for t in range(nb):
            slot = (nb - 1 - t) % 2
            if t + LOOKAHEAD < nb:
                sts[t + LOOKAHEAD] = tuple(logits(offs[t + LOOKAHEAD], e) for e in range(2))
            for e in range(2):
                softmax_step(sts[t][e], e, slot, block_shift(first - t, e))
            if t < nb - 1:
                for e in range(2):
                    pv_step(offs[t], e, slot)

    rem = (nlive + 1) % UNROLL
    nb = 1
    while nb < UNROLL:
        below = rem & (nb - 1)
        pl.when(jnp.logical_and((rem & nb) != 0, below == 0))(
            functools.partial(step, i, nb, True))
        if nb > 1:
            pl.when(jnp.logical_and((rem & nb) != 0, below != 0))(
                functools.partial(step, i - below, nb, False))
        nb *= 2
    pl.when(rem == 0)(functools.partial(step, i, UNROLL, True))
    head = jnp.where(rem == 0, UNROLL, rem)

    def body(k, carry):
        step(i - head - UNROLL * k, UNROLL, False)
        return carry

    lax.fori_loop(0, (nlive + 1 - head) // UNROLL, body, 0)
    last = pl.multiple_of((i - nlive) * TK, TK)
    for e in range(2):
        pv_step(last, e, 0)
    ot = jnp.concatenate([acc_ref[e] / stat_ref[e, 1:2, :] for e in range(2)], axis=0)
    o_ref[pl.ds(qoff, tq), :] = ot.T.astype(o_ref.dtype)
    return nlive_next


def _fox_attention(thr, cref, qt, ka, vt, ct, batch, seq):
    tq = TK
    npair = D_FOX // LANES
    nh = D_FOX // HEAD_DIM
    smem = pl.BlockSpec(memory_space=pltpu.SMEM)
    return pl.pallas_call(
        functools.partial(_attn_kernel, nblk=seq // TK),
        out_shape=jax.ShapeDtypeStruct((batch, seq, D_FOX), BF16),
        grid=(batch, npair),
        in_specs=[
            smem, smem,
            pl.BlockSpec((LANES, seq), lambda b, p: (p, b)),
            pl.BlockSpec((seq, 2 * LANES), lambda b, p: (b, p)),
            pl.BlockSpec((LANES, seq), lambda b, p: (p, b)),
            pl.BlockSpec((nh, seq), lambda b, p: (0, b)),
        ],
        out_specs=pl.BlockSpec((None, seq, LANES), lambda b, p: (b, 0, p)),
        scratch_shapes=[pltpu.VMEM((2, HEAD_DIM, tq), F32), pltpu.VMEM((2, 2, TK, tq), BF16),
                        pltpu.VMEM((2, 8, tq), F32)],
        compiler_params=_params(("parallel", "parallel")),
        name="fox_attn",
    )(thr, cref, qt, ka, vt, ct)


def _out_proj_kernel(h_ref, y1_ref, y2_ref, w1_ref, w2_ref, o_ref):
    o_ref[...] = h_ref[...] + _dot(y1_ref[...], w1_ref[...]) + _dot(y2_ref[...], w2_ref[...])


def _resident(shape):
    return pl.BlockSpec(shape, lambda i: (0,) * len(shape), pipeline_mode=pl.Buffered(1))


def _out_proj(h, y1, y2, w1, w2, *, tm):
    t, d = h.shape
    return pl.pallas_call(
        _out_proj_kernel,
        out_shape=jax.ShapeDtypeStruct((t, d), F32),
        grid=(t // tm,),
        in_specs=[
            pl.BlockSpec((tm, d), lambda i: (i, 0)),
            pl.BlockSpec((tm, y1.shape[1]), lambda i: (i, 0)),
            pl.BlockSpec((tm, y2.shape[1]), lambda i: (i, 0)),
            _resident(w1.shape),
            _resident(w2.shape),
        ],
        out_specs=pl.BlockSpec((tm, d), lambda i: (i, 0)),
        compiler_params=_params(("parallel",)),
        name="out_proj",
    )(h, y1, y2, w1, w2)


def _mlp_kernel(x_ref, g_ref, w1_ref, w2_ref, o_ref, xn_ref):
    @pl.when(pl.program_id(1) == 0)
    def _():
        x = x_ref[...]
        xn_ref[...] = _rms(x, g_ref[...]).astype(BF16)
        o_ref[...] = x

    hid = jnp.maximum(_dot(xn_ref[...], w1_ref[...]), 0.0)
    o_ref[...] += _dot((hid * hid).astype(BF16), w2_ref[...])


def _mlp(h, g, w1, w2, *, tm, tf):
    t, d = h.shape
    f = w1.shape[1]
    return pl.pallas_call(
        _mlp_kernel,
        out_shape=jax.ShapeDtypeStruct((t, d), F32),
        grid=(t // tm, f // tf),
        in_specs=[
            pl.BlockSpec((tm, d), lambda i, j: (i, 0)),
            pl.BlockSpec((1, d), lambda i, j: (0, 0)),
            pl.BlockSpec((d, tf), lambda i, j: (0, j)),
            pl.BlockSpec((tf, d), lambda i, j: (j, 0)),
        ],
        out_specs=pl.BlockSpec((tm, d), lambda i, j: (i, 0)),
        scratch_shapes=[pltpu.VMEM((tm, d), BF16)],
        compiler_params=_params(("parallel", "arbitrary")),
        name="mlp",
    )(h, g, w1, w2)


def _ple_kernel(x_ref, g_ref, p_ref, wg_ref, wp_ref, gf_ref, o_ref, *, final):
    x = x_ref[...]
    gate = _sigmoid(_dot(_rms(x, g_ref[...]).astype(BF16), wg_ref[...]))
    h = x + gate * _dot(p_ref[...].astype(BF16), wp_ref[...])
    o_ref[...] = _rms(h, gf_ref[...]) if final else h


def _ple(h, g, p, wg, wp, g_final, *, tm, final):
    t, d = h.shape
    dp = p.shape[1]
    return pl.pallas_call(
        functools.partial(_ple_kernel, final=final),
        out_shape=jax.ShapeDtypeStruct((t, d), F32),
        grid=(t // tm,),
        in_specs=[
            pl.BlockSpec((tm, d), lambda i: (i, 0)),
            _resident((1, d)),
            pl.BlockSpec((tm, dp), lambda i: (i, 0)),
            _resident(wg.shape),
            _resident(wp.shape),
            _resident((1, d)),
        ],
        out_specs=pl.BlockSpec((tm, d), lambda i: (i, 0)),
        compiler_params=_params(("parallel",)),
        name="ple",
    )(h, g, p, wg, wp, g_final)


def _block_diag_ones(n):
    idx = jnp.arange(n) // HEAD_DIM
    return (idx[:, None] == idx[None, :]).astype(BF16)


def _pad_cols(w, n):
    return jnp.pad(w, ((0, 0), (0, n - w.shape[1])))


def _take_heads(w, order, axis):
    shape = w.shape
    split = shape[:axis] + (order.shape[0], shape[axis] // order.shape[0]) + shape[axis + 1:]
    return jnp.take(w.reshape(split), order, axis=axis).reshape(shape)


def _pack_w_in(w, order):
    n_shift = 3 * D_RWKV + N_LORA
    rkv = w[:, :3 * D_RWKV]
    lora = _pad_cols(w[:, 3 * D_RWKV:n_shift], N_LORA_PAD)
    fox = [_take_heads(w[:, n_shift + k * D_FOX:n_shift + (k + 1) * D_FOX], order, 1) for k in range(3)]
    ff = _pad_cols(jnp.take(w[:, n_shift + 3 * D_FOX:], order, axis=1), N_FF_PAD)
    return jnp.concatenate([rkv] + fox + [lora, ff], axis=1).astype(BF16)


def _bias_lane_tables():
    nh = D_FOX // HEAD_DIM
    h = jnp.arange(nh)
    lane = (h // 2) * LANES + (h % 2) * AUG
    sel = jnp.zeros((3, N_FF_PAD, D_FOX), F32)
    one = jnp.zeros((1, D_FOX), F32)
    for k in range(3):
        sel = sel.at[k, h, lane + k].set(1.0)
        one = one.at[0, lane + 3 + k].set(1.0)
    return sel.astype(BF16), one


def _skip_threshold(gq, gk):
    smax = 1.02 * 8.0 * jnp.max(jnp.abs(gq)) * jnp.max(jnp.abs(gk)) * LOG2E
    return (-(EXP2_UNDERFLOW + 2.0 * smax)).reshape(1).astype(F32)


def _block_starts(ct, batch, seq):
    nh = ct.shape[0]
    ends = ct.reshape(nh, batch, seq // TK, TK)[..., TK - 1]
    starts = jnp.pad(ends, ((0, 0), (0, 0), (1, 0)))
    return starts.transpose(1, 0, 2).reshape(-1)


def _lora_rows(w, start):
    return jnp.pad(w, ((start, N_LORA_PAD - start - w.shape[0]), (0, 0))).astype(BF16)


def kernel(x, p, norm_mix, w_in, mu_shift, w0, w2, a0, a2, g2, k_k, k_a, r_k, gn_w, gn_b, b_f, q_norm, k_norm, w_out, norm_mlp, w_ff1, w_ff2, norm_ple, w_ple_gate, w_ple_proj, norm_final):
    batch, seq, d = x.shape
    depth = w_in.shape[0]
    t = batch * seq
    nh_fox = D_FOX // HEAD_DIM
    tm = min(512, seq)
    sel, one = _bias_lane_tables()
    bd256 = _block_diag_ones(256)
    bd128 = _block_diag_ones(LANES)
    row = lambda a: a.reshape(1, -1)

    h = x.reshape(t, d)
    for i in range(depth):
        order = jnp.argsort(b_f[i])
        z = _in_proj(h, row(norm_mix[i]), _pack_w_in(w_in[i], order), tm=min(1024, seq), tn=512)

        mu = mu_shift[i]
        feats = _rwkv_prep(
            z, seq, row(mu[:3 * D_RWKV]), row(jnp.pad(mu[3 * D_RWKV:], (0, N_LORA_PAD - N_LORA))),
            row(w0[i]), _lora_rows(w2[i], 0), row(a0[i]), _lora_rows(a2[i], 64),
            _lora_rows(g2[i], 128), row(k_k[i]), row(k_a[i]), bd256, tm=min(256, seq))
        y_rw = _rwkv_scan(feats[:6], feats[6], row(gn_w[i]), row(gn_b[i]), row(r_k[i]), bd256,
                          batch, seq, tc=min(512, seq))

        qt, ka, vt, ct = _fox_prep(
            z, seq, row(jnp.tile(q_norm[i], nh_fox)), row(jnp.tile(k_norm[i], nh_fox)),
            row(jnp.pad(jnp.take(b_f[i], order), (0, N_FF_PAD - nh_fox))), bd256, sel, one)
        y_fox = _fox_attention(_skip_threshold(q_norm[i], k_norm[i]), _block_starts(ct, batch, seq),
                               qt, ka, vt, ct, batch, seq)

        wo = w_out[i].astype(BF16)
        h = _out_proj(h, y_rw.reshape(t, D_RWKV), y_fox.reshape(t, D_FOX), wo[:D_RWKV],
                      _take_heads(wo[D_RWKV:], order, 0), tm=tm)
        h = _mlp(h, row(norm_mlp[i]), w_ff1[i].astype(BF16), w_ff2[i].astype(BF16), tm=tm, tf=512)
        h = _ple(h, row(norm_ple[i]), p[i].reshape(t, -1), w_ple_gate[i].astype(BF16),
                 w_ple_proj[i].astype(BF16), row(norm_final), tm=tm, final=i == depth - 1)
    return h.reshape(batch, seq, d)
```

```python
import functools

import numpy as np
import jax
import jax.numpy as jnp
from jax import lax
from jax.experimental import pallas as pl
from jax.experimental.pallas import tpu as pltpu

F32 = jnp.float32
BF16 = jnp.bfloat16

HEAD_DIM = 64
D_RWKV = 1024
D_FOX = 1024
N_LORA = 288
N_LORA_PAD = 384
N_FF_PAD = 128
NORM_EPS = 1e-6
GN_EPS = 64e-5
CHUNK = 64
LANES = 128
TK = 256
AUG = 8
LOG2E = 1.4426950408889634
EXP2_UNDERFLOW = 152.0
UNROLL = 8
LOOKAHEAD = 2
VMEM_LIMIT = 56 * 1024 * 1024


def _dot(a, b):
    return jnp.dot(a, b, preferred_element_type=F32)


def _dot_nt(a, b):
    return lax.dot_general(a, b, (((1,), (1,)), ((), ())), preferred_element_type=F32)


def _dot_tn(a, b):
    return lax.dot_general(a, b, (((0,), (0,)), ((), ())), preferred_element_type=F32)


def _split3(x):
    h1 = x.astype(BF16)
    r1 = x - h1.astype(F32)
    h2 = r1.astype(BF16)
    h3 = (r1 - h2.astype(F32)).astype(BF16)
    return h1, h2, h3


def _split2(x):
    h1 = x.astype(BF16)
    h2 = (x - h1.astype(F32)).astype(BF16)
    return h1, h2


def _group_sum(x, bd):
    w = bd.shape[0]
    hi, lo = _split2(x)
    outs = []
    for c in range(x.shape[1] // w):
        sl = slice(c * w, (c + 1) * w)
        outs.append(_dot(hi[:, sl], bd) + _dot(lo[:, sl], bd))
    return outs[0] if len(outs) == 1 else jnp.concatenate(outs, axis=1)


def _group_sums(xs, bd):
    rows = xs[0].shape[0]
    parts = []
    for x in xs:
        parts.extend(_split2(x))
    out = _dot(jnp.concatenate(parts, axis=0), bd)
    return [out[2 * k * rows:(2 * k + 1) * rows] + out[(2 * k + 1) * rows:(2 * k + 2) * rows]
            for k in range(len(xs))]


def _cumsum_rows(tri, x):
    h1, h2, h3 = _split3(x)
    return _dot(tri, h1) + _dot(tri, h2) + _dot(tri, h3)


def _sigmoid(x):
    return 1.0 / (1.0 + jnp.exp(-x))


def _softplus(x):
    return jnp.maximum(x, 0.0) + jnp.log1p(jnp.exp(-jnp.abs(x)))


def _rms(x, g):
    ms = jnp.mean(x * x, axis=-1, keepdims=True)
    return x * lax.rsqrt(ms + NORM_EPS) * g


def _params(sem):
    return pltpu.CompilerParams(dimension_semantics=sem, vmem_limit_bytes=VMEM_LIMIT)


def _in_proj_kernel(x_ref, g_ref, w_ref, o_ref, xn_ref):
    @pl.when(pl.program_id(1) == 0)
    def _():
        xn_ref[...] = _rms(x_ref[...], g_ref[...]).astype(BF16)

    o_ref[...] = _dot(xn_ref[...], w_ref[...])


def _in_proj(h, g, w, *, tm, tn):
    t, d = h.shape
    n = w.shape[1]
    return pl.pallas_call(
        _in_proj_kernel,
        out_shape=jax.ShapeDtypeStruct((t, n), F32),
        grid=(t // tm, n // tn),
        in_specs=[
            pl.BlockSpec((tm, d), lambda i, j: (i, 0)),
            pl.BlockSpec((1, d), lambda i, j: (0, 0)),
            pl.BlockSpec((d, tn), lambda i, j: (0, j)),
        ],
        out_specs=pl.BlockSpec((tm, tn), lambda i, j: (i, j)),
        scratch_shapes=[pltpu.VMEM((tm, d), BF16)],
        compiler_params=_params(("parallel", "arbitrary")),
        name="in_proj",
    )(h, g, w)


def _rwkv_prep_kernel(zr_ref, zl_ref, pr_ref, pl_ref, mur_ref, mul_ref, w0_ref, w2_ref, a0_ref,
                      a2_ref, g2_ref, kkw_ref, kaw_ref, bd_ref,
                      r_out, ld_out, kp_out, v_out, kk_out, b_out, g_out, *, blocks_per_seq):
    first = (pl.program_id(0) % blocks_per_seq) == 0

    def shift(z, prev8, mu):
        last = jnp.where(first, 0.0, prev8[7:8, :])
        rolled = pltpu.roll(z, 1, 0)
        row = lax.broadcasted_iota(jnp.int32, z.shape, 0)
        prev = jnp.where(row == 0, last, rolled)
        return z + (prev - z) * mu

    zs = shift(zr_ref[...], pr_ref[...], mur_ref[...])
    zl = shift(zl_ref[...], pl_ref[...], mul_ref[...])
    r = zs[:, :D_RWKV]
    k = zs[:, D_RWKV:2 * D_RWKV]
    v = zs[:, 2 * D_RWKV:]

    lw = w0_ref[...] + _dot(jnp.tanh(zl).astype(BF16), w2_ref[...])
    w_log2 = ((jnp.minimum(lw, 0.0) - 0.5) * LOG2E
              - jnp.log2(1.0 + jnp.exp2(jnp.abs(lw) * -LOG2E)))
    ld = -jnp.exp2(w_log2)
    a = _sigmoid(a0_ref[...] + _dot(zl.astype(BF16), a2_ref[...]))
    g = _dot(_sigmoid(zl).astype(BF16), g2_ref[...])

    kkr = k * kkw_ref[...]
    ss = _group_sum(kkr * kkr, bd_ref[...])
    kk = kkr * lax.rsqrt(jnp.maximum(ss, 1e-24))
    kp = k * (1.0 + (a - 1.0) * kaw_ref[...])

    r_out[...] = r.astype(r_out.dtype)
    ld_out[...] = ld
    kp_out[...] = kp.astype(kp_out.dtype)
    v_out[...] = v.astype(v_out.dtype)
    kk_out[...] = kk.astype(kk_out.dtype)
    b_out[...] = (kk * a).astype(b_out.dtype)
    g_out[...] = g.astype(g_out.dtype)


def _rwkv_prep(z, seq, mu_r, mu_l, w0, w2p, a0, a2p, g2p, k_k, k_a, bd, *, tm):
    t = z.shape[0]
    c_lora = (3 * D_RWKV + 3 * D_FOX) // N_LORA_PAD
    rows8 = tm // 8
    row = lambda i: (0, 0)
    prev = lambda i: (jnp.maximum(i * rows8 - 1, 0), 0)
    out = [jax.ShapeDtypeStruct((t, D_RWKV), F32 if name == "ld" else BF16)
           for name in ("r", "ld", "kp", "v", "kk", "b", "g")]
    return pl.pallas_call(
        functools.partial(_rwkv_prep_kernel, blocks_per_seq=seq // tm),
        out_shape=out,
        grid=(t // tm,),
        in_specs=[
            pl.BlockSpec((tm, 3 * D_RWKV), lambda i: (i, 0)),
            pl.BlockSpec((tm, N_LORA_PAD), lambda i: (i, c_lora)),
            pl.BlockSpec((8, 3 * D_RWKV), prev),
            pl.BlockSpec((8, N_LORA_PAD), lambda i: (jnp.maximum(i * rows8 - 1, 0), c_lora)),
            pl.BlockSpec((1, 3 * D_RWKV), row),
            pl.BlockSpec((1, N_LORA_PAD), row),
            pl.BlockSpec((1, D_RWKV), row),
            pl.BlockSpec((N_LORA_PAD, D_RWKV), row),
            pl.BlockSpec((1, D_RWKV), row),
            pl.BlockSpec((N_LORA_PAD, D_RWKV), row),
            pl.BlockSpec((N_LORA_PAD, D_RWKV), row),
            pl.BlockSpec((1, D_RWKV), row),
            pl.BlockSpec((1, D_RWKV), row),
            pl.BlockSpec(bd.shape, row),
        ],
        out_specs=[pl.BlockSpec((tm, D_RWKV), lambda i: (i, 0))] * 7,
        compiler_params=_params(("parallel",)),
        name="rwkv_prep",
    )(z, z, z, z, mu_r, mu_l, w0, w2p, a0, a2p, g2p, k_k, k_a, bd)


def _stack_heads(q, masks):
    zero = jnp.zeros_like(q)
    return jnp.concatenate([jnp.where(m, q, zero) for m in masks], axis=0).astype(BF16)


def _diag_blocks(full, masks):
    out = full[:HEAD_DIM]
    for g in range(1, len(masks)):
        out = jnp.where(masks[g], full[g * HEAD_DIM:(g + 1) * HEAD_DIM], out)
    return out


def _scan_kernel(r_ref, ld_ref, kp_ref, v_ref, kk_ref, b_ref, g_ref, gnw_ref, gnb_ref, rk_ref,
                 bd_ref, y_ref, s_ref, *, nchunk):
    L = CHUNK
    W = s_ref.shape[1]
    chunks = range(nchunk)

    @pl.when(pl.program_id(2) == 0)
    def _():
        s_ref[...] = jnp.zeros_like(s_ref)

    lane = lax.broadcasted_iota(jnp.int32, (1, W), 1)
    masks = [lane // HEAD_DIM == g for g in range(W // HEAD_DIM)]
    row = lax.broadcasted_iota(jnp.int32, (L, W), 0)
    col = lax.broadcasted_iota(jnp.int32, (L, W), 1) % HEAD_DIM
    strict = row > col
    incl = row >= col
    eye = (row == col).astype(F32)
    tri = (lax.broadcasted_iota(jnp.int32, (L, L), 0)
           >= lax.broadcasted_iota(jnp.int32, (L, L), 1)).astype(BF16)
    bd = bd_ref[...]
    zero = jnp.zeros((L, W), F32)

    def stack(q):
        return _stack_heads(q, masks)

    def hmm(p, qst):
        return _dot(p.astype(BF16), qst)

    sl = [slice(c * L, (c + 1) * L) for c in chunks]
    ld = [ld_ref[s, :] for s in sl]
    cum = [_cumsum_rows(tri, x) for x in ld]
    w_last = [jnp.exp(x[L - 1:L, :]) for x in cum]
    w_inv = [jnp.exp(-x) for x in cum]
    r = [r_ref[s, :].astype(F32) for s in sl]
    kp = [kp_ref[s, :].astype(F32) for s in sl]
    v = [v_ref[s, :].astype(F32) for s in sl]
    abar = [-kk_ref[sl[c], :].astype(F32) * jnp.exp(cum[c] - ld[c]) for c in chunks]
    rbar = [r[c] * jnp.exp(cum[c]) for c in chunks]
    btil = [b_ref[sl[c], :].astype(F32) * w_inv[c] for c in chunks]
    ktil = [kp[c] * w_inv[c] for c in chunks]
    bhat = [btil[c] * w_last[c] for c in chunks]
    khat = [ktil[c] * w_last[c] for c in chunks]

    a_all = [_dot_nt(jnp.concatenate([abar[c], rbar[c]], axis=0).astype(BF16),
                     jnp.concatenate([stack(btil[c]), stack(ktil[c])], axis=0))
             for c in chunks]
    n = [jnp.where(strict, a[:L, :W], zero) for a in a_all]
    a_ak = [jnp.where(strict, a[:L, W:], zero) for a in a_all]
    a_rb = [jnp.where(incl, a[L:, :W], zero) for a in a_all]
    a_rk = [jnp.where(incl, a[L:, W:], zero) for a in a_all]

    vst = [stack(x) for x in v]
    av = [hmm(jnp.concatenate([a_ak[c], a_rk[c]], axis=0), vst[c]) for c in chunks]

    tinv = [eye + x for x in n]
    nk = [hmm(x, stack(x)) for x in n]
    for _ in range(4):
        both = [hmm(jnp.concatenate([tinv[c], nk[c]], axis=0), stack(nk[c])) for c in chunks]
        tinv = [tinv[c] + both[c][:L] for c in chunks]
        nk = [x[L:] for x in both]
    tinv = [tinv[c] + hmm(tinv[c], stack(nk[c])) for c in chunks]

    at = [hmm(tinv[c], stack(abar[c])) for c in chunks]
    u0 = [hmm(tinv[c], stack(av[c][:L])) for c in chunks]
    rt = [(rbar[c] + hmm(a_rb[c], stack(at[c]))).astype(BF16) for c in chunks]
    y0 = [hmm(a_rb[c], stack(u0[c])) + av[c][L:] for c in chunks]
    pst = [stack(_diag_blocks(_dot_tn(at[c].astype(BF16), bhat[c].astype(BF16)), masks))
           for c in chunks]
    q = [_diag_blocks(_dot_tn(jnp.concatenate([u0[c], v[c]], axis=0).astype(BF16),
                              jnp.concatenate([bhat[c], khat[c]], axis=0).astype(BF16)), masks)
         for c in chunks]

    s = s_ref[...]
    ys = []
    for c in chunks:
        ys.append(_dot_nt(rt[c], stack(s)) + y0[c])
        s = s * w_last[c] + hmm(s, pst[c]) + q[c]
    s_ref[...] = s

    sums = _group_sums(ys + [r[c] * kp[c] * rk_ref[...] for c in chunks], bd)
    ds = [ys[c] - sums[c] * (1.0 / HEAD_DIM) for c in chunks]
    var = _group_sums([d * d for d in ds], bd)
    for c in chunks:
        yn = ds[c] * lax.rsqrt(var[c] * (1.0 / HEAD_DIM) + GN_EPS) * gnw_ref[...] + gnb_ref[...]
        bonus = sums[nchunk + c] * v[c]
        y_ref[sl[c], :] = ((yn + bonus) * g_ref[sl[c], :].astype(F32)).astype(y_ref.dtype)


def _rwkv_scan(feats, g, gn_w, gn_b, r_k, bd, batch, seq, *, tc):
    gw = bd.shape[0]
    tok = pl.BlockSpec((None, tc, gw), lambda b, p, c: (b, c, p))
    par = pl.BlockSpec((1, gw), lambda b, p, c: (0, p))
    feats = [f.reshape(batch, seq, D_RWKV) for f in feats]
    return pl.pallas_call(
        functools.partial(_scan_kernel, nchunk=tc // CHUNK),
        out_shape=jax.ShapeDtypeStruct((batch, seq, D_RWKV), BF16),
        grid=(batch, D_RWKV // gw, seq // tc),
        in_specs=[tok] * 7 + [par] * 3 + [pl.BlockSpec((gw, gw), lambda b, p, c: (0, 0))],
        out_specs=tok,
        scratch_shapes=[pltpu.VMEM((HEAD_DIM, gw), F32)],
        compiler_params=_params(("parallel", "parallel", "arbitrary")),
        name="rwkv_scan",
    )(*feats, g.reshape(batch, seq, D_RWKV), gn_w, gn_b, r_k, bd)


def _fox_prep_kernel(q_ref, k_ref, v_ref, ff_ref, gq_ref, gk_ref, bf_ref, bd_ref, sel_ref, one_ref,
                     qt_ref, ka_ref, vt_ref, ct_ref, carry_ref, *, blocks_per_seq):
    @pl.when((pl.program_id(0) % blocks_per_seq) == 0)
    def _():
        carry_ref[...] = jnp.zeros_like(carry_ref)

    bd = bd_ref[...]
    q = q_ref[...]
    k = k_ref[...]
    msq = _group_sum(q * q, bd) * (1.0 / HEAD_DIM)
    msk = _group_sum(k * k, bd) * (1.0 / HEAD_DIM)
    qn = q * lax.rsqrt(msq + NORM_EPS) * gq_ref[...] * (HEAD_DIM ** -0.5 * LOG2E)
    kn = k * lax.rsqrt(msk + NORM_EPS) * gk_ref[...]
    qt_ref[...] = qn.T.astype(BF16)
    vt_ref[...] = v_ref[...].T.astype(BF16)

    tm = q.shape[0]
    log_f = -_softplus(-(ff_ref[...] + bf_ref[...])) * LOG2E
    tri = (lax.broadcasted_iota(jnp.int32, (tm, tm), 0)
           >= lax.broadcasted_iota(jnp.int32, (tm, tm), 1)).astype(BF16)
    c_local = _cumsum_rows(tri, log_f)
    c = c_local + carry_ref[...]
    carry_ref[...] = c[tm - 1:tm, :]
    ct_ref[...] = c.T[:ct_ref.shape[0], :]

    n1, n2, n3 = _split3(-c_local)
    aug = _dot(n1, sel_ref[0]) + _dot(n2, sel_ref[1]) + _dot(n3, sel_ref[2]) + one_ref[...]
    for p in range(D_FOX // LANES):
        src = slice(p * LANES, (p + 1) * LANES)
        ka_ref[:, 2 * p * LANES:(2 * p + 1) * LANES] = kn[:, src].astype(BF16)
        ka_ref[:, (2 * p + 1) * LANES:(2 * p + 2) * LANES] = aug[:, src].astype(BF16)


def _fox_prep(z, seq, gq, gk, bf, bd, sel, one):
    t = z.shape[0]
    tm = TK
    nh = D_FOX // HEAD_DIM
    c_ff = (3 * D_RWKV + 3 * D_FOX + N_LORA_PAD) // N_FF_PAD
    row = lambda i: (0, 0)
    tr = jax.ShapeDtypeStruct((D_FOX, t), BF16)
    return pl.pallas_call(
        functools.partial(_fox_prep_kernel, blocks_per_seq=seq // tm),
        out_shape=[tr, jax.ShapeDtypeStruct((t, 2 * D_FOX), BF16), tr,
                   jax.ShapeDtypeStruct((nh, t), F32)],
        grid=(t // tm,),
        in_specs=[
            pl.BlockSpec((tm, D_FOX), lambda i: (i, 3)),
            pl.BlockSpec((tm, D_FOX), lambda i: (i, 4)),
            pl.BlockSpec((tm, D_FOX), lambda i: (i, 5)),
            pl.BlockSpec((tm, N_FF_PAD), lambda i: (i, c_ff)),
            pl.BlockSpec((1, D_FOX), row),
            pl.BlockSpec((1, D_FOX), row),
            pl.BlockSpec((1, N_FF_PAD), row),
            pl.BlockSpec(bd.shape, row),
            pl.BlockSpec(sel.shape, lambda i: (0, 0, 0)),
            pl.BlockSpec((1, D_FOX), row),
        ],
        out_specs=[pl.BlockSpec((D_FOX, tm), lambda i: (0, i)),
                   pl.BlockSpec((tm, 2 * D_FOX), lambda i: (i, 0)),
                   pl.BlockSpec((D_FOX, tm), lambda i: (0, i)),
                   pl.BlockSpec((nh, tm), lambda i: (0, i))],
        scratch_shapes=[pltpu.VMEM((1, N_FF_PAD), F32)],
        compiler_params=_params(("arbitrary",)),
        name="fox_prep",
    )(z, z, z, z, gq, gk, bf, bd, sel, one)


def _attn_kernel(thr_ref, cref_ref, qt_ref, ka_ref, vt_ref, ct_ref, o_ref, acc_ref, pt_ref,
                 stat_ref, *, nblk):
    def qblock(i, nlive):
        return _attn_qblock(i, nlive, thr_ref, cref_ref, qt_ref, ka_ref, vt_ref, ct_ref, o_ref,
                            acc_ref, pt_ref, stat_ref, nblk=nblk)

    lax.fori_loop(0, nblk, qblock, jnp.int32(0))


def _live_blocks(i, thr, cref_ref, base, nblk):
    ci = [cref_ref[b + i] for b in base]
    lo = [jnp.int32(0)] * len(base)
    hi = [i] * len(base)
    for _ in range(nblk.bit_length()):
        for e, b in enumerate(base):
            mid = (lo[e] + hi[e]) // 2
            live = ci[e] - cref_ref[b + i - mid] >= thr
            is_open = lo[e] < hi[e]
            lo[e], hi[e] = (jnp.where(jnp.logical_and(is_open, live), mid + 1, lo[e]),
                            jnp.where(jnp.logical_and(is_open, jnp.logical_not(live)), mid, hi[e]))
    return functools.reduce(jnp.maximum, lo)


def _attn_qblock(i, nlive, thr_ref, cref_ref, qt_ref, ka_ref, vt_ref, ct_ref, o_ref, acc_ref,
                 pt_ref, stat_ref, *, nblk):
    b = pl.program_id(0)
    p = pl.program_id(1)
    nh = ct_ref.shape[0]
    tq = TK
    qoff = pl.multiple_of(i * tq, tq)

    row_d = lax.broadcasted_iota(jnp.int32, (LANES, 1), 0)
    qt = qt_ref[:, pl.ds(qoff, tq)]
    zq = jnp.zeros_like(qt)
    qte = (jnp.where(row_d < HEAD_DIM, qt, zq), jnp.where(row_d < HEAD_DIM, zq, qt))
    rs = lax.broadcasted_iota(jnp.int32, (16, tq), 0)
    zpad = jnp.zeros((LANES - 16, tq), BF16)
    base = tuple((b * nh + 2 * p + e) * (nblk + 1) for e in range(2))
    ci = tuple(cref_ref[base[e] + i] for e in range(2))

    def rhs_operand(e):
        d = ct_ref[pl.ds(2 * p + e, 1), pl.ds(qoff, tq)] - ci[e]
        d1 = d.astype(BF16).astype(F32)
        r1 = d - d1
        d2 = r1.astype(BF16).astype(F32)
        d3 = r1 - d2
        ones3 = ((rs >= AUG * e) & (rs < AUG * e + 3)).astype(F32)
        strip = jnp.where(rs == AUG * e + 3, d1,
                          jnp.where(rs == AUG * e + 4, d2,
                                    jnp.where(rs == AUG * e + 5, d3, ones3)))
        return jnp.concatenate([qte[e], strip.astype(BF16), zpad], axis=0)

    rhs = tuple(rhs_operand(e) for e in range(2))

    def logits(off, e):
        return _dot(ka_ref[pl.ds(off, TK), :], rhs[e])

    def block_shift(j, e):
        return ci[e] - cref_ref[base[e] + j]

    def col_reduce(x, op):
        return op(op(x.reshape(8, x.shape[0] // 8, x.shape[1]), axis=0), axis=0, keepdims=True)

    def softmax_step(st, e, slot, shift):
        m = stat_ref[e, 0:1, :]
        m_new = jnp.maximum(m, col_reduce(st, jnp.max) + shift)
        alpha = jnp.exp2(m - m_new)
        pt = jnp.exp2(st - (m_new - shift))
        pt_ref[slot, e] = pt.astype(BF16)
        stat_ref[e, 0:1, :] = m_new
        stat_ref[e, 1:2, :] = alpha * stat_ref[e, 1:2, :] + col_reduce(pt, jnp.sum)
        stat_ref[e, 2 + slot:3 + slot, :] = alpha

    def pv_step(off, e, slot):
        vte = vt_ref[pl.ds(HEAD_DIM * e, HEAD_DIM), pl.ds(off, TK)]
        acc_ref[e] = (stat_ref[e, 2 + slot:3 + slot, :] * acc_ref[e]
                      + _dot(vte, pt_ref[slot, e]))

    acc_ref[...] = jnp.zeros_like(acc_ref)
    for e in range(2):
        stat_ref[e, 0:1, :] = jnp.full((1, tq), -jnp.inf, F32)
        stat_ref[e, 1:2, :] = jnp.zeros((1, tq), F32)

    thr = thr_ref[0]

    nlive_next = _live_blocks(i + 1, thr, cref_ref, base, nblk)

    def step(first, nb, diagonal):
        offs = [pl.multiple_of((first - t) * TK, TK) for t in range(nb)]
        sts = [None] * nb
        for t in range(min(LOOKAHEAD, nb)):
            sts[t] = tuple(logits(offs[t], e) for e in range(2))
        if diagonal:
            tt = lax.broadcasted_iota(jnp.int32, (TK, tq), 0)
            rr = lax.broadcasted_iota(jnp.int32, (TK, tq), 1)
            sts[0] = tuple(jnp.where(tt <= rr, st, -jnp.inf) for st in sts[0])
        else:
            for e in range(2):
                pv_step(pl.multiple_of(offs[0] + TK, TK), e, 0)
        for t in range(nb):
            slot = (nb - 1 - t) % 2
            if t + LOOKAHEAD < nb:
                sts[t + LOOKAHEAD] = tuple(logits(offs[t + LOOKAHEAD], e) for e in range(2))
            for e in range(2):
                softmax_step(sts[t][e], e, slot, block_shift(first - t, e))
            if t < nb - 1:
                for e in range(2):
                    pv_step(offs[t], e, slot)

    rem = (nlive + 1) % UNROLL
    nb = 1
    while nb < UNROLL:
        below = rem & (nb - 1)
        pl.when(jnp.logical_and((rem & nb) != 0, below == 0))(
            functools.partial(step, i, nb, True))
        if nb > 1:
            pl.when(jnp.logical_and((rem & nb) != 0, below != 0))(
                functools.partial(step, i - below, nb, False))
        nb *= 2
    pl.when(rem == 0)(functools.partial(step, i, UNROLL, True))
    head = jnp.where(rem == 0, UNROLL, rem)

    def body(k, carry):
        step(i - head - UNROLL * k, UNROLL, False)
        return carry

    lax.fori_loop(0, (nlive + 1 - head) // UNROLL, body, 0)
    last = pl.multiple_of((i - nlive) * TK, TK)
    for e in range(2):
        pv_step(last, e, 0)
    ot = jnp.concatenate([acc_ref[e] / stat_ref[e, 1:2, :] for e in range(2)], axis=0)
    o_ref[pl.ds(qoff, tq), :] = ot.T.astype(o_ref.dtype)
    return nlive_next


def _fox_attention(thr, cref, qt, ka, vt, ct, batch, seq):
    tq = TK
    npair = D_FOX // LANES
    nh = D_FOX // HEAD_DIM
    smem = pl.BlockSpec(memory_space=pltpu.SMEM)
    return pl.pallas_call(
        functools.partial(_attn_kernel, nblk=seq // TK),
        out_shape=jax.ShapeDtypeStruct((batch, seq, D_FOX), BF16),
        grid=(batch, npair),
        in_specs=[
            smem, smem,
            pl.BlockSpec((LANES, seq), lambda b, p: (p, b)),
            pl.BlockSpec((seq, 2 * LANES), lambda b, p: (b, p)),
            pl.BlockSpec((LANES, seq), lambda b, p: (p, b)),
            pl.BlockSpec((nh, seq), lambda b, p: (0, b)),
        ],
        out_specs=pl.BlockSpec((None, seq, LANES), lambda b, p: (b, 0, p)),
        scratch_shapes=[pltpu.VMEM((2, HEAD_DIM, tq), F32), pltpu.VMEM((2, 2, TK, tq), BF16),
                        pltpu.VMEM((2, 8, tq), F32)],
        compiler_params=_params(("parallel", "parallel")),
        name="fox_attn",
    )(thr, cref, qt, ka, vt, ct)


def _out_proj_kernel(h_ref, y1_ref, y2_ref, w1_ref, w2_ref, o_ref):
    o_ref[...] = h_ref[...] + _dot(y1_ref[...], w1_ref[...]) + _dot(y2_ref[...], w2_ref[...])


def _resident(shape):
    return pl.BlockSpec(shape, lambda i: (0,) * len(shape), pipeline_mode=pl.Buffered(1))


def _out_proj(h, y1, y2, w1, w2, *, tm):
    t, d = h.shape
    return pl.pallas_call(
        _out_proj_kernel,
        out_shape=jax.ShapeDtypeStruct((t, d), F32),
        grid=(t // tm,),
        in_specs=[
            pl.BlockSpec((tm, d), lambda i: (i, 0)),
            pl.BlockSpec((tm, y1.shape[1]), lambda i: (i, 0)),
            pl.BlockSpec((tm, y2.shape[1]), lambda i: (i, 0)),
            _resident(w1.shape),
            _resident(w2.shape),
        ],
        out_specs=pl.BlockSpec((tm, d), lambda i: (i, 0)),
        compiler_params=_params(("parallel",)),
        name="out_proj",
    )(h, y1, y2, w1, w2)


def _mlp_kernel(x_ref, g_ref, w1_ref, w2_ref, o_ref, xn_ref):
    @pl.when(pl.program_id(1) == 0)
    def _():
        x = x_ref[...]
        xn_ref[...] = _rms(x, g_ref[...]).astype(BF16)
        o_ref[...] = x

    hid = jnp.maximum(_dot(xn_ref[...], w1_ref[...]), 0.0)
    o_ref[...] += _dot((hid * hid).astype(BF16), w2_ref[...])


def _mlp(h, g, w1, w2, *, tm, tf):
    t, d = h.shape
    f = w1.shape[1]
    return pl.pallas_call(
        _mlp_kernel,
        out_shape=jax.ShapeDtypeStruct((t, d), F32),
        grid=(t // tm, f // tf),
        in_specs=[
            pl.BlockSpec((tm, d), lambda i, j: (i, 0)),
            pl.BlockSpec((1, d), lambda i, j: (0, 0)),
            pl.BlockSpec((d, tf), lambda i, j: (0, j)),
            pl.BlockSpec((tf, d), lambda i, j: (j, 0)),
        ],
        out_specs=pl.BlockSpec((tm, d), lambda i, j: (i, 0)),
        scratch_shapes=[pltpu.VMEM((tm, d), BF16)],
        compiler_params=_params(("parallel", "arbitrary")),
        name="mlp",
    )(h, g, w1, w2)


def _ple_kernel(x_ref, g_ref, p_ref, wg_ref, wp_ref, gf_ref, o_ref, *, final):
    x = x_ref[...]
    gate = _sigmoid(_dot(_rms(x, g_ref[...]).astype(BF16), wg_ref[...]))
    h = x + gate * _dot(p_ref[...].astype(BF16), wp_ref[...])
    o_ref[...] = _rms(h, gf_ref[...]) if final else h


def _ple(h, g, p, wg, wp, g_final, *, tm, final):
    t, d = h.shape
    dp = p.shape[1]
    return pl.pallas_call(
        functools.partial(_ple_kernel, final=final),
        out_shape=jax.ShapeDtypeStruct((t, d), F32),
        grid=(t // tm,),
        in_specs=[
            pl.BlockSpec((tm, d), lambda i: (i, 0)),
            _resident((1, d)),
            pl.BlockSpec((tm, dp), lambda i: (i, 0)),
            _resident(wg.shape),
            _resident(wp.shape),
            _resident((1, d)),
        ],
        out_specs=pl.BlockSpec((tm, d), lambda i: (i, 0)),
        compiler_params=_params(("parallel",)),
        name="ple",
    )(h, g, p, wg, wp, g_final)


def _block_diag_ones(n):
    idx = np.arange(n) // HEAD_DIM
    return jnp.asarray(idx[:, None] == idx[None, :], BF16)


def _pad_cols(w, n):
    return jnp.pad(w, ((0, 0), (0, n - w.shape[1])))


def _take_heads(w, order, axis):
    shape = w.shape
    split = shape[:axis] + (order.shape[0], shape[axis] // order.shape[0]) + shape[axis + 1:]
    return jnp.take(w.reshape(split), order, axis=axis).reshape(shape)


def _pack_w_in(w, order):
    n_shift = 3 * D_RWKV + N_LORA
    rkv = w[:, :3 * D_RWKV]
    lora = _pad_cols(w[:, 3 * D_RWKV:n_shift], N_LORA_PAD)
    fox = [_take_heads(w[:, n_shift + k * D_FOX:n_shift + (k + 1) * D_FOX], order, 1) for k in range(3)]
    ff = _pad_cols(jnp.take(w[:, n_shift + 3 * D_FOX:], order, axis=1), N_FF_PAD)
    return jnp.concatenate([rkv] + fox + [lora, ff], axis=1).astype(BF16)


def _bias_lane_tables():
    nh = D_FOX // HEAD_DIM
    h = np.arange(nh)
    lane = (h // 2) * LANES + (h % 2) * AUG
    sel = np.zeros((3, N_FF_PAD, D_FOX), np.float32)
    one = np.zeros((1, D_FOX), np.float32)
    for k in range(3):
        sel[k, h, lane + k] = 1.0
        one[0, lane + 3 + k] = 1.0
    return jnp.asarray(sel, BF16), jnp.asarray(one)


def _skip_threshold(gq, gk):
    smax = 1.02 * 8.0 * jnp.max(jnp.abs(gq)) * jnp.max(jnp.abs(gk)) * LOG2E
    return (-(EXP2_UNDERFLOW + 2.0 * smax)).reshape(1).astype(F32)


def _block_starts(ct, batch, seq):
    nh = ct.shape[0]
    ends = ct.reshape(nh, batch, seq // TK, TK)[..., TK - 1]
    starts = jnp.pad(ends, ((0, 0), (0, 0), (1, 0)))
    return starts.transpose(1, 0, 2).reshape(-1)


def _lora_rows(w, start):
    return jnp.pad(w, ((start, N_LORA_PAD - start - w.shape[0]), (0, 0))).astype(BF16)


def kernel(x, p, norm_mix, w_in, mu_shift, w0, w2, a0, a2, g2, k_k, k_a, r_k, gn_w, gn_b, b_f, q_norm, k_norm, w_out, norm_mlp, w_ff1, w_ff2, norm_ple, w_ple_gate, w_ple_proj, norm_final):
    batch, seq, d = x.shape
    depth = w_in.shape[0]
    t = batch * seq
    nh_fox = D_FOX // HEAD_DIM
    tm = min(512, seq)
    sel, one = _bias_lane_tables()
    bd256 = _block_diag_ones(256)
    bd128 = _block_diag_ones(LANES)
    row = lambda a: a.reshape(1, -1)

    h = x.reshape(t, d)
    for i in range(depth):
        order = jnp.argsort(b_f[i])
        z = _in_proj(h, row(norm_mix[i]), _pack_w_in(w_in[i], order), tm=min(1024, seq), tn=512)

        mu = mu_shift[i]
        feats = _rwkv_prep(
            z, seq, row(mu[:3 * D_RWKV]), row(jnp.pad(mu[3 * D_RWKV:], (0, N_LORA_PAD - N_LORA))),
            row(w0[i]), _lora_rows(w2[i], 0), row(a0[i]), _lora_rows(a2[i], 64),
            _lora_rows(g2[i], 128), row(k_k[i]), row(k_a[i]), bd256, tm=min(256, seq))
        y_rw = _rwkv_scan(feats[:6], feats[6], row(gn_w[i]), row(gn_b[i]), row(r_k[i]), bd256,
                          batch, seq, tc=min(1024, seq))

        qt, ka, vt, ct = _fox_prep(
            z, seq, row(jnp.tile(q_norm[i], nh_fox)), row(jnp.tile(k_norm[i], nh_fox)),
            row(jnp.pad(jnp.take(b_f[i], order), (0, N_FF_PAD - nh_fox))), bd256, sel, one)
        y_fox = _fox_attention(_skip_threshold(q_norm[i], k_norm[i]), _block_starts(ct, batch, seq),
                               qt, ka, vt, ct, batch, seq)

        wo = w_out[i].astype(BF16)
        h = _out_proj(h, y_rw.reshape(t, D_RWKV), y_fox.reshape(t, D_FOX), wo[:D_RWKV],
                      _take_heads(wo[D_RWKV:], order, 0), tm=tm)
        h = _mlp(h, row(norm_mlp[i]), w_ff1[i].astype(BF16), w_ff2[i].astype(BF16), tm=tm, tf=512)
        h = _ple(h, row(norm_ple[i]), p[i].reshape(t, -1), w_ple_gate[i].astype(BF16),
                 w_ple_proj[i].astype(BF16), row(norm_final), tm=tm, final=i == depth - 1)
    return h.reshape(batch, seq, d)
```

```python
import functools

import numpy as np
import jax
import jax.numpy as jnp
from jax import lax
from jax.experimental import pallas as pl
from jax.experimental.pallas import tpu as pltpu

F32 = jnp.float32
BF16 = jnp.bfloat16

HEAD_DIM = 64
D_RWKV = 1024
D_FOX = 1024
N_LORA = 288
N_LORA_PAD = 384
N_FF_PAD = 128
NORM_EPS = 1e-6
GN_EPS = 64e-5
CHUNK = 64
LANES = 128
TK = 256
AUG = 8
LOG2E = 1.4426950408889634
EXP2_UNDERFLOW = 152.0
UNROLL = 8
LOOKAHEAD = 2
VMEM_LIMIT = 56 * 1024 * 1024


def _dot(a, b):
    return jnp.dot(a, b, preferred_element_type=F32)


def _dot_nt(a, b):
    return lax.dot_general(a, b, (((1,), (1,)), ((), ())), preferred_element_type=F32)


def _dot_tn(a, b):
    return lax.dot_general(a, b, (((0,), (0,)), ((), ())), preferred_element_type=F32)


def _split3(x):
    h1 = x.astype(BF16)
    r1 = x - h1.astype(F32)
    h2 = r1.astype(BF16)
    h3 = (r1 - h2.astype(F32)).astype(BF16)
    return h1, h2, h3


def _split2(x):
    h1 = x.astype(BF16)
    h2 = (x - h1.astype(F32)).astype(BF16)
    return h1, h2


def _group_sum(x, bd):
    w = bd.shape[0]
    hi, lo = _split2(x)
    outs = []
    for c in range(x.shape[1] // w):
        sl = slice(c * w, (c + 1) * w)
        outs.append(_dot(hi[:, sl], bd) + _dot(lo[:, sl], bd))
    return outs[0] if len(outs) == 1 else jnp.concatenate(outs, axis=1)


def _group_sums(xs, bd):
    rows = xs[0].shape[0]
    parts = []
    for x in xs:
        parts.extend(_split2(x))
    out = _dot(jnp.concatenate(parts, axis=0), bd)
    return [out[2 * k * rows:(2 * k + 1) * rows] + out[(2 * k + 1) * rows:(2 * k + 2) * rows]
            for k in range(len(xs))]


def _cumsum_rows(tri, x):
    h1, h2, h3 = _split3(x)
    return _dot(tri, h1) + _dot(tri, h2) + _dot(tri, h3)


def _sigmoid(x):
    return 1.0 / (1.0 + jnp.exp(-x))


def _softplus(x):
    return jnp.maximum(x, 0.0) + jnp.log1p(jnp.exp(-jnp.abs(x)))


def _rms(x, g):
    ms = jnp.mean(x * x, axis=-1, keepdims=True)
    return x * lax.rsqrt(ms + NORM_EPS) * g


def _params(sem):
    return pltpu.CompilerParams(dimension_semantics=sem, vmem_limit_bytes=VMEM_LIMIT)


def _in_proj_kernel(x_ref, g_ref, w_ref, o_ref, xn_ref):
    @pl.when(pl.program_id(1) == 0)
    def _():
        xn_ref[...] = _rms(x_ref[...], g_ref[...]).astype(BF16)

    o_ref[...] = _dot(xn_ref[...], w_ref[...])


def _in_proj(h, g, w, *, tm, tn):
    t, d = h.shape
    n = w.shape[1]
    return pl.pallas_call(
        _in_proj_kernel,
        out_shape=jax.ShapeDtypeStruct((t, n), F32),
        grid=(t // tm, n // tn),
        in_specs=[
            pl.BlockSpec((tm, d), lambda i, j: (i, 0)),
            pl.BlockSpec((1, d), lambda i, j: (0, 0)),
            pl.BlockSpec((d, tn), lambda i, j: (0, j)),
        ],
        out_specs=pl.BlockSpec((tm, tn), lambda i, j: (i, j)),
        scratch_shapes=[pltpu.VMEM((tm, d), BF16)],
        compiler_params=_params(("parallel", "arbitrary")),
        name="in_proj",
    )(h, g, w)


def _rwkv_prep_kernel(zr_ref, zl_ref, pr_ref, pl_ref, mur_ref, mul_ref, w0_ref, w2_ref, a0_ref,
                      a2_ref, g2_ref, kkw_ref, kaw_ref, bd_ref,
                      r_out, ld_out, kp_out, v_out, kk_out, b_out, g_out, *, blocks_per_seq):
    first = (pl.program_id(0) % blocks_per_seq) == 0

    def shift(z, prev8, mu):
        last = jnp.where(first, 0.0, prev8[7:8, :])
        rolled = pltpu.roll(z, 1, 0)
        row = lax.broadcasted_iota(jnp.int32, z.shape, 0)
        prev = jnp.where(row == 0, last, rolled)
        return z + (prev - z) * mu

    zs = shift(zr_ref[...], pr_ref[...], mur_ref[...])
    zl = shift(zl_ref[...], pl_ref[...], mul_ref[...])
    r = zs[:, :D_RWKV]
    k = zs[:, D_RWKV:2 * D_RWKV]
    v = zs[:, 2 * D_RWKV:]

    lw = w0_ref[...] + _dot(jnp.tanh(zl).astype(BF16), w2_ref[...])
    w_log2 = ((jnp.minimum(lw, 0.0) - 0.5) * LOG2E
              - jnp.log2(1.0 + jnp.exp2(jnp.abs(lw) * -LOG2E)))
    ld = -jnp.exp2(w_log2)
    a = _sigmoid(a0_ref[...] + _dot(zl.astype(BF16), a2_ref[...]))
    g = _dot(_sigmoid(zl).astype(BF16), g2_ref[...])

    kkr = k * kkw_ref[...]
    ss = _group_sum(kkr * kkr, bd_ref[...])
    kk = kkr * lax.rsqrt(jnp.maximum(ss, 1e-24))
    kp = k * (1.0 + (a - 1.0) * kaw_ref[...])

    r_out[...] = r.astype(r_out.dtype)
    ld_out[...] = ld
    kp_out[...] = kp.astype(kp_out.dtype)
    v_out[...] = v.astype(v_out.dtype)
    kk_out[...] = kk.astype(kk_out.dtype)
    b_out[...] = (kk * a).astype(b_out.dtype)
    g_out[...] = g.astype(g_out.dtype)


def _rwkv_prep(z, seq, mu_r, mu_l, w0, w2p, a0, a2p, g2p, k_k, k_a, bd, *, tm):
    t = z.shape[0]
    c_lora = (3 * D_RWKV + 3 * D_FOX) // N_LORA_PAD
    rows8 = tm // 8
    row = lambda i: (0, 0)
    prev = lambda i: (jnp.maximum(i * rows8 - 1, 0), 0)
    out = [jax.ShapeDtypeStruct((t, D_RWKV), F32 if name == "ld" else BF16)
           for name in ("r", "ld", "kp", "v", "kk", "b", "g")]
    return pl.pallas_call(
        functools.partial(_rwkv_prep_kernel, blocks_per_seq=seq // tm),
        out_shape=out,
        grid=(t // tm,),
        in_specs=[
            pl.BlockSpec((tm, 3 * D_RWKV), lambda i: (i, 0)),
            pl.BlockSpec((tm, N_LORA_PAD), lambda i: (i, c_lora)),
            pl.BlockSpec((8, 3 * D_RWKV), prev),
            pl.BlockSpec((8, N_LORA_PAD), lambda i: (jnp.maximum(i * rows8 - 1, 0), c_lora)),
            pl.BlockSpec((1, 3 * D_RWKV), row),
            pl.BlockSpec((1, N_LORA_PAD), row),
            pl.BlockSpec((1, D_RWKV), row),
            pl.BlockSpec((N_LORA_PAD, D_RWKV), row),
            pl.BlockSpec((1, D_RWKV), row),
            pl.BlockSpec((N_LORA_PAD, D_RWKV), row),
            pl.BlockSpec((N_LORA_PAD, D_RWKV), row),
            pl.BlockSpec((1, D_RWKV), row),
            pl.BlockSpec((1, D_RWKV), row),
            pl.BlockSpec(bd.shape, row),
        ],
        out_specs=[pl.BlockSpec((tm, D_RWKV), lambda i: (i, 0))] * 7,
        compiler_params=_params(("parallel",)),
        name="rwkv_prep",
    )(z, z, z, z, mu_r, mu_l, w0, w2p, a0, a2p, g2p, k_k, k_a, bd)


def _stack_heads(q, masks):
    zero = jnp.zeros_like(q)
    return jnp.concatenate([jnp.where(m, q, zero) for m in masks], axis=0).astype(BF16)


def _diag_blocks(full, masks):
    out = full[:HEAD_DIM]
    for g in range(1, len(masks)):
        out = jnp.where(masks[g], full[g * HEAD_DIM:(g + 1) * HEAD_DIM], out)
    return out


def _scan_kernel(r_ref, ld_ref, kp_ref, v_ref, kk_ref, b_ref, g_ref, gnw_ref, gnb_ref, rk_ref,
                 bd_ref, y_ref, s_ref, *, nchunk):
    L = CHUNK
    W = s_ref.shape[1]
    chunks = range(nchunk)

    @pl.when(pl.program_id(2) == 0)
    def _():
        s_ref[...] = jnp.zeros_like(s_ref)

    lane = lax.broadcasted_iota(jnp.int32, (1, W), 1)
    masks = [lane // HEAD_DIM == g for g in range(W // HEAD_DIM)]
    row = lax.broadcasted_iota(jnp.int32, (L, W), 0)
    col = lax.broadcasted_iota(jnp.int32, (L, W), 1) % HEAD_DIM
    strict = row > col
    incl = row >= col
    eye = (row == col).astype(F32)
    tri = (lax.broadcasted_iota(jnp.int32, (L, L), 0)
           >= lax.broadcasted_iota(jnp.int32, (L, L), 1)).astype(BF16)
    bd = bd_ref[...]
    zero = jnp.zeros((L, W), F32)

    def stack(q):
        return _stack_heads(q, masks)

    def hmm(p, qst):
        return _dot(p.astype(BF16), qst)

    sl = [slice(c * L, (c + 1) * L) for c in chunks]
    ld = [ld_ref[s, :] for s in sl]
    cum = [_cumsum_rows(tri, x) for x in ld]
    w_last = [jnp.exp(x[L - 1:L, :]) for x in cum]
    w_inv = [jnp.exp(-x) for x in cum]
    r = [r_ref[s, :].astype(F32) for s in sl]
    kp = [kp_ref[s, :].astype(F32) for s in sl]
    v = [v_ref[s, :].astype(F32) for s in sl]
    abar = [-kk_ref[sl[c], :].astype(F32) * jnp.exp(cum[c] - ld[c]) for c in chunks]
    rbar = [r[c] * jnp.exp(cum[c]) for c in chunks]
    btil = [b_ref[sl[c], :].astype(F32) * w_inv[c] for c in chunks]
    ktil = [kp[c] * w_inv[c] for c in chunks]
    bhat = [btil[c] * w_last[c] for c in chunks]
    khat = [ktil[c] * w_last[c] for c in chunks]

    a_all = [_dot_nt(jnp.concatenate([abar[c], rbar[c]], axis=0).astype(BF16),
                     jnp.concatenate([stack(btil[c]), stack(ktil[c])], axis=0))
             for c in chunks]
    n = [jnp.where(strict, a[:L, :W], zero) for a in a_all]
    a_ak = [jnp.where(strict, a[:L, W:], zero) for a in a_all]
    a_rb = [jnp.where(incl, a[L:, :W], zero) for a in a_all]
    a_rk = [jnp.where(incl, a[L:, W:], zero) for a in a_all]

    vst = [stack(x) for x in v]
    av = [hmm(jnp.concatenate([a_ak[c], a_rk[c]], axis=0), vst[c]) for c in chunks]

    tinv = [eye + x for x in n]
    nk = [hmm(x, stack(x)) for x in n]
    for _ in range(4):
        both = [hmm(jnp.concatenate([tinv[c], nk[c]], axis=0), stack(nk[c])) for c in chunks]
        tinv = [tinv[c] + both[c][:L] for c in chunks]
        nk = [x[L:] for x in both]
    tinv = [tinv[c] + hmm(tinv[c], stack(nk[c])) for c in chunks]

    at = [hmm(tinv[c], stack(abar[c])) for c in chunks]
    u0 = [hmm(tinv[c], stack(av[c][:L])) for c in chunks]
    rt = [(rbar[c] + hmm(a_rb[c], stack(at[c]))).astype(BF16) for c in chunks]
    y0 = [hmm(a_rb[c], stack(u0[c])) + av[c][L:] for c in chunks]
    pst = [stack(_diag_blocks(_dot_tn(at[c].astype(BF16), bhat[c].astype(BF16)), masks))
           for c in chunks]
    q = [_diag_blocks(_dot_tn(jnp.concatenate([u0[c], v[c]], axis=0).astype(BF16),
                              jnp.concatenate([bhat[c], khat[c]], axis=0).astype(BF16)), masks)
         for c in chunks]

    s = s_ref[...]
    ys = []
    for c in chunks:
        ys.append(_dot_nt(rt[c], stack(s)) + y0[c])
        s = s * w_last[c] + hmm(s, pst[c]) + q[c]
    s_ref[...] = s

    sums = _group_sums(ys + [r[c] * kp[c] * rk_ref[...] for c in chunks], bd)
    ds = [ys[c] - sums[c] * (1.0 / HEAD_DIM) for c in chunks]
    var = _group_sums([d * d for d in ds], bd)
    for c in chunks:
        yn = ds[c] * lax.rsqrt(var[c] * (1.0 / HEAD_DIM) + GN_EPS) * gnw_ref[...] + gnb_ref[...]
        bonus = sums[nchunk + c] * v[c]
        y_ref[sl[c], :] = ((yn + bonus) * g_ref[sl[c], :].astype(F32)).astype(y_ref.dtype)


def _rwkv_scan(feats, g, gn_w, gn_b, r_k, bd, batch, seq, *, tc):
    gw = bd.shape[0]
    tok = pl.BlockSpec((None, tc, gw), lambda b, p, c: (b, c, p))
    par = pl.BlockSpec((1, gw), lambda b, p, c: (0, p))
    feats = [f.reshape(batch, seq, D_RWKV) for f in feats]
    return pl.pallas_call(
        functools.partial(_scan_kernel, nchunk=tc // CHUNK),
        out_shape=jax.ShapeDtypeStruct((batch, seq, D_RWKV), BF16),
        grid=(batch, D_RWKV // gw, seq // tc),
        in_specs=[tok] * 7 + [par] * 3 + [pl.BlockSpec((gw, gw), lambda b, p, c: (0, 0))],
        out_specs=tok,
        scratch_shapes=[pltpu.VMEM((HEAD_DIM, gw), F32)],
        compiler_params=_params(("parallel", "parallel", "arbitrary")),
        name="rwkv_scan",
    )(*feats, g.reshape(batch, seq, D_RWKV), gn_w, gn_b, r_k, bd)


def _fox_prep_kernel(q_ref, k_ref, v_ref, ff_ref, gq_ref, gk_ref, bf_ref, bd_ref, sel_ref, one_ref,
                     qt_ref, ka_ref, vt_ref, ct_ref, carry_ref, *, blocks_per_seq):
    @pl.when((pl.program_id(0) % blocks_per_seq) == 0)
    def _():
        carry_ref[...] = jnp.zeros_like(carry_ref)

    bd = bd_ref[...]
    q = q_ref[...]
    k = k_ref[...]
    msq = _group_sum(q * q, bd) * (1.0 / HEAD_DIM)
    msk = _group_sum(k * k, bd) * (1.0 / HEAD_DIM)
    qn = q * lax.rsqrt(msq + NORM_EPS) * gq_ref[...] * (HEAD_DIM ** -0.5 * LOG2E)
    kn = k * lax.rsqrt(msk + NORM_EPS) * gk_ref[...]
    qt_ref[...] = qn.T.astype(BF16)
    vt_ref[...] = v_ref[...].T.astype(BF16)

    tm = q.shape[0]
    log_f = -_softplus(-(ff_ref[...] + bf_ref[...])) * LOG2E
    tri = (lax.broadcasted_iota(jnp.int32, (tm, tm), 0)
           >= lax.broadcasted_iota(jnp.int32, (tm, tm), 1)).astype(BF16)
    c_local = _cumsum_rows(tri, log_f)
    c = c_local + carry_ref[...]
    carry_ref[...] = c[tm - 1:tm, :]
    ct_ref[...] = c.T[:ct_ref.shape[0], :]

    n1, n2, n3 = _split3(-c_local)
    aug = _dot(n1, sel_ref[0]) + _dot(n2, sel_ref[1]) + _dot(n3, sel_ref[2]) + one_ref[...]
    for p in range(D_FOX // LANES):
        src = slice(p * LANES, (p + 1) * LANES)
        ka_ref[:, 2 * p * LANES:(2 * p + 1) * LANES] = kn[:, src].astype(BF16)
        ka_ref[:, (2 * p + 1) * LANES:(2 * p + 2) * LANES] = aug[:, src].astype(BF16)


def _fox_prep(z, seq, gq, gk, bf, bd, sel, one):
    t = z.shape[0]
    tm = TK
    nh = D_FOX // HEAD_DIM
    c_ff = (3 * D_RWKV + 3 * D_FOX + N_LORA_PAD) // N_FF_PAD
    row = lambda i: (0, 0)
    tr = jax.ShapeDtypeStruct((D_FOX, t), BF16)
    return pl.pallas_call(
        functools.partial(_fox_prep_kernel, blocks_per_seq=seq // tm),
        out_shape=[tr, jax.ShapeDtypeStruct((t, 2 * D_FOX), BF16), tr,
                   jax.ShapeDtypeStruct((nh, t), F32)],
        grid=(t // tm,),
        in_specs=[
            pl.BlockSpec((tm, D_FOX), lambda i: (i, 3)),
            pl.BlockSpec((tm, D_FOX), lambda i: (i, 4)),
            pl.BlockSpec((tm, D_FOX), lambda i: (i, 5)),
            pl.BlockSpec((tm, N_FF_PAD), lambda i: (i, c_ff)),
            pl.BlockSpec((1, D_FOX), row),
            pl.BlockSpec((1, D_FOX), row),
            pl.BlockSpec((1, N_FF_PAD), row),
            pl.BlockSpec(bd.shape, row),
            pl.BlockSpec(sel.shape, lambda i: (0, 0, 0)),
            pl.BlockSpec((1, D_FOX), row),
        ],
        out_specs=[pl.BlockSpec((D_FOX, tm), lambda i: (0, i)),
                   pl.BlockSpec((tm, 2 * D_FOX), lambda i: (i, 0)),
                   pl.BlockSpec((D_FOX, tm), lambda i: (0, i)),
                   pl.BlockSpec((nh, tm), lambda i: (0, i))],
        scratch_shapes=[pltpu.VMEM((1, N_FF_PAD), F32)],
        compiler_params=_params(("arbitrary",)),
        name="fox_prep",
    )(z, z, z, z, gq, gk, bf, bd, sel, one)


def _attn_kernel(thr_ref, cref_ref, qt_ref, ka_ref, vt_ref, ct_ref, o_ref, acc_ref, pt_ref,
                 stat_ref, *, nblk):
    def qblock(i, nlive):
        return _attn_qblock(i, nlive, thr_ref, cref_ref, qt_ref, ka_ref, vt_ref, ct_ref, o_ref,
                            acc_ref, pt_ref, stat_ref, nblk=nblk)

    lax.fori_loop(0, nblk, qblock, jnp.int32(0))


def _live_blocks(i, thr, cref_ref, base, nblk):
    ci = [cref_ref[b + i] for b in base]
    lo = [jnp.int32(0)] * len(base)
    hi = [i] * len(base)
    for _ in range(nblk.bit_length()):
        for e, b in enumerate(base):
            mid = (lo[e] + hi[e]) // 2
            live = ci[e] - cref_ref[b + i - mid] >= thr
            is_open = lo[e] < hi[e]
            lo[e], hi[e] = (jnp.where(jnp.logical_and(is_open, live), mid + 1, lo[e]),
                            jnp.where(jnp.logical_and(is_open, jnp.logical_not(live)), mid, hi[e]))
    return functools.reduce(jnp.maximum, lo)


def _attn_qblock(i, nlive, thr_ref, cref_ref, qt_ref, ka_ref, vt_ref, ct_ref, o_ref, acc_ref,
                 pt_ref, stat_ref, *, nblk):
    b = pl.program_id(0)
    p = pl.program_id(1)
    nh = ct_ref.shape[0]
    tq = TK
    qoff = pl.multiple_of(i * tq, tq)

    row_d = lax.broadcasted_iota(jnp.int32, (LANES, 1), 0)
    qt = qt_ref[:, pl.ds(qoff, tq)]
    zq = jnp.zeros_like(qt)
    qte = (jnp.where(row_d < HEAD_DIM, qt, zq), jnp.where(row_d < HEAD_DIM, zq, qt))
    rs = lax.broadcasted_iota(jnp.int32, (16, tq), 0)
    zpad = jnp.zeros((LANES - 16, tq), BF16)
    base = tuple((b * nh + 2 * p + e) * (nblk + 1) for e in range(2))
    ci = tuple(cref_ref[base[e] + i] for e in range(2))

    def rhs_operand(e):
        d = ct_ref[pl.ds(2 * p + e, 1), pl.ds(qoff, tq)] - ci[e]
        d1 = d.astype(BF16).astype(F32)
        r1 = d - d1
        d2 = r1.astype(BF16).astype(F32)
        d3 = r1 - d2
        ones3 = ((rs >= AUG * e) & (rs < AUG * e + 3)).astype(F32)
        strip = jnp.where(rs == AUG * e + 3, d1,
                          jnp.where(rs == AUG * e + 4, d2,
                                    jnp.where(rs == AUG * e + 5, d3, ones3)))
        return jnp.concatenate([qte[e], strip.astype(BF16), zpad], axis=0)

    rhs = tuple(rhs_operand(e) for e in range(2))

    def logits(off, e):
        return _dot(ka_ref[pl.ds(off, TK), :], rhs[e])

    def block_shift(j, e):
        return ci[e] - cref_ref[base[e] + j]

    def col_reduce(x, op):
        return op(op(x.reshape(8, x.shape[0] // 8, x.shape[1]), axis=0), axis=0, keepdims=True)

    def softmax_step(st, e, slot, shift):
        m = stat_ref[e, 0:1, :]
        m_new = jnp.maximum(m, col_reduce(st, jnp.max) + shift)
        alpha = jnp.exp2(m - m_new)
        pt = jnp.exp2(st - (m_new - shift))
        pt_ref[slot, e] = pt.astype(BF16)
        stat_ref[e, 0:1, :] = m_new
        stat_ref[e, 1:2, :] = alpha * stat_ref[e, 1:2, :] + col_reduce(pt, jnp.sum)
        stat_ref[e, 2 + slot:3 + slot, :] = alpha

    def pv_step(off, e, slot):
        vte = vt_ref[pl.ds(HEAD_DIM * e, HEAD_DIM), pl.ds(off, TK)]
        acc_ref[e] = (stat_ref[e, 2 + slot:3 + slot, :] * acc_ref[e]
                      + _dot(vte, pt_ref[slot, e]))

    acc_ref[...] = jnp.zeros_like(acc_ref)
    for e in range(2):
        stat_ref[e, 0:1, :] = jnp.full((1, tq), -jnp.inf, F32)
        stat_ref[e, 1:2, :] = jnp.zeros((1, tq), F32)

    thr = thr_ref[0]

    nlive_next = _live_blocks(i + 1, thr, cref_ref, base, nblk)

    def step(first, nb, diagonal):
        offs = [pl.multiple_of((first - t) * TK, TK) for t in range(nb)]
        sts = [None] * nb
        for t in range(min(LOOKAHEAD, nb)):
            sts[t] = tuple(logits(offs[t], e) for e in range(2))
        if diagonal:
            tt = lax.broadcasted_iota(jnp.int32, (TK, tq), 0)
            rr = lax.broadcasted_iota(jnp.int32, (TK, tq), 1)
            sts[0] = tuple(jnp.where(tt <= rr, st, -jnp.inf) for st in sts[0])
        else:
            for e in range(2):
                pv_step(pl.multiple_of(offs[0] + TK, TK), e, 0)
        for t in range(nb):
            slot = (nb - 1 - t) % 2
            if t + LOOKAHEAD < nb:
                sts[t + LOOKAHEAD] = tuple(logits(offs[t + LOOKAHEAD], e) for e in range(2))
            for e in range(2):
                softmax_step(sts[t][e], e, slot, block_shift(first - t, e))
            if t < nb - 1:
                for e in range(2):
                    pv_step(offs[t], e, slot)

    rem = (nlive + 1) % UNROLL
    nb = 1
    while nb < UNROLL:
        below = rem & (nb - 1)
        pl.when(jnp.logical_and((rem & nb) != 0, below == 0))(
            functools.partial(step, i, nb, True))
        if nb > 1:
            pl.when(jnp.logical_and((rem & nb) != 0, below != 0))(
                functools.partial(step, i - below, nb, False))
        nb *= 2
    pl.when(rem == 0)(functools.partial(step, i, UNROLL, True))
    head = jnp.where(rem == 0, UNROLL, rem)

    def body(k, carry):
        step(i - head - UNROLL * k, UNROLL, False)
        return carry

    lax.fori_loop(0, (nlive + 1 - head) // UNROLL, body, 0)
    last = pl.multiple_of((i - nlive) * TK, TK)
    for e in range(2):
        pv_step(last, e, 0)
    ot = jnp.concatenate([acc_ref[e] / stat_ref[e, 1:2, :] for e in range(2)], axis=0)
    o_ref[pl.ds(qoff, tq), :] = ot.T.astype(o_ref.dtype)
    return nlive_next


def _fox_attention(thr, cref, qt, ka, vt, ct, batch, seq):
    tq = TK
    npair = D_FOX // LANES
    nh = D_FOX // HEAD_DIM
    smem = pl.BlockSpec(memory_space=pltpu.SMEM)
    return pl.pallas_call(
        functools.partial(_attn_kernel, nblk=seq // TK),
        out_shape=jax.ShapeDtypeStruct((batch, seq, D_FOX), BF16),
        grid=(batch, npair),
        in_specs=[
            smem, smem,
            pl.BlockSpec((LANES, seq), lambda b, p: (p, b)),
            pl.BlockSpec((seq, 2 * LANES), lambda b, p: (b, p)),
            pl.BlockSpec((LANES, seq), lambda b, p: (p, b)),
            pl.BlockSpec((nh, seq), lambda b, p: (0, b)),
        ],
        out_specs=pl.BlockSpec((None, seq, LANES), lambda b, p: (b, 0, p)),
        scratch_shapes=[pltpu.VMEM((2, HEAD_DIM, tq), F32), pltpu.VMEM((2, 2, TK, tq), BF16),
                        pltpu.VMEM((2, 8, tq), F32)],
        compiler_params=_params(("parallel", "parallel")),
        name="fox_attn",
    )(thr, cref, qt, ka, vt, ct)


def _out_proj_kernel(h_ref, y1_ref, y2_ref, w1_ref, w2_ref, o_ref):
    o_ref[...] = h_ref[...] + _dot(y1_ref[...], w1_ref[...]) + _dot(y2_ref[...], w2_ref[...])


def _resident(shape):
    return pl.BlockSpec(shape, lambda i: (0,) * len(shape), pipeline_mode=pl.Buffered(1))


def _out_proj(h, y1, y2, w1, w2, *, tm):
    t, d = h.shape
    return pl.pallas_call(
        _out_proj_kernel,
        out_shape=jax.ShapeDtypeStruct((t, d), F32),
        grid=(t // tm,),
        in_specs=[
            pl.BlockSpec((tm, d), lambda i: (i, 0)),
            pl.BlockSpec((tm, y1.shape[1]), lambda i: (i, 0)),
            pl.BlockSpec((tm, y2.shape[1]), lambda i: (i, 0)),
            _resident(w1.shape),
            _resident(w2.shape),
        ],
        out_specs=pl.BlockSpec((tm, d), lambda i: (i, 0)),
        compiler_params=_params(("parallel",)),
        name="out_proj",
    )(h, y1, y2, w1, w2)


def _mlp_kernel(x_ref, g_ref, w1_ref, w2_ref, o_ref, xn_ref):
    @pl.when(pl.program_id(1) == 0)
    def _():
        x = x_ref[...]
        xn_ref[...] = _rms(x, g_ref[...]).astype(BF16)
        o_ref[...] = x

    hid = jnp.maximum(_dot(xn_ref[...], w1_ref[...]), 0.0)
    o_ref[...] += _dot((hid * hid).astype(BF16), w2_ref[...])


def _mlp(h, g, w1, w2, *, tm, tf):
    t, d = h.shape
    f = w1.shape[1]
    return pl.pallas_call(
        _mlp_kernel,
        out_shape=jax.ShapeDtypeStruct((t, d), F32),
        grid=(t // tm, f // tf),
        in_specs=[
            pl.BlockSpec((tm, d), lambda i, j: (i, 0)),
            pl.BlockSpec((1, d), lambda i, j: (0, 0)),
            pl.BlockSpec((d, tf), lambda i, j: (0, j)),
            pl.BlockSpec((tf, d), lambda i, j: (j, 0)),
        ],
        out_specs=pl.BlockSpec((tm, d), lambda i, j: (i, 0)),
        scratch_shapes=[pltpu.VMEM((tm, d), BF16)],
        compiler_params=_params(("parallel", "arbitrary")),
        name="mlp",
    )(h, g, w1, w2)


def _ple_kernel(x_ref, g_ref, p_ref, wg_ref, wp_ref, gf_ref, o_ref, *, final):
    x = x_ref[...]
    gate = _sigmoid(_dot(_rms(x, g_ref[...]).astype(BF16), wg_ref[...]))
    h = x + gate * _dot(p_ref[...].astype(BF16), wp_ref[...])
    o_ref[...] = _rms(h, gf_ref[...]) if final else h


def _ple(h, g, p, layer, wg, wp, g_final, *, tm, final):
    t, d = h.shape
    dp = p.shape[2]
    return pl.pallas_call(
        functools.partial(_ple_kernel, final=final),
        out_shape=jax.ShapeDtypeStruct((t, d), F32),
        grid=(t // tm,),
        in_specs=[
            pl.BlockSpec((tm, d), lambda i: (i, 0)),
            _resident((1, d)),
            pl.BlockSpec((None, tm, dp), lambda i: (layer, i, 0)),
            _resident(wg.shape),
            _resident(wp.shape),
            _resident((1, d)),
        ],
        out_specs=pl.BlockSpec((tm, d), lambda i: (i, 0)),
        compiler_params=_params(("parallel",)),
        name="ple",
    )(h, g, p, wg, wp, g_final)


def _block_diag_ones(n):
    idx = np.arange(n) // HEAD_DIM
    return jnp.asarray(idx[:, None] == idx[None, :], BF16)


def _pad_cols(w, n):
    return jnp.pad(w, ((0, 0), (0, n - w.shape[1])))


def _take_heads(w, order, axis):
    shape = w.shape
    split = shape[:axis] + (order.shape[0], shape[axis] // order.shape[0]) + shape[axis + 1:]
    return jnp.take(w.reshape(split), order, axis=axis).reshape(shape)


def _pack_w_in(w, order):
    n_shift = 3 * D_RWKV + N_LORA
    rkv = w[:, :3 * D_RWKV]
    lora = _pad_cols(w[:, 3 * D_RWKV:n_shift], N_LORA_PAD)
    fox = [_take_heads(w[:, n_shift + k * D_FOX:n_shift + (k + 1) * D_FOX], order, 1) for k in range(3)]
    ff = _pad_cols(jnp.take(w[:, n_shift + 3 * D_FOX:], order, axis=1), N_FF_PAD)
    return jnp.concatenate([rkv] + fox + [lora, ff], axis=1).astype(BF16)


def _bias_lane_tables():
    nh = D_FOX // HEAD_DIM
    h = np.arange(nh)
    lane = (h // 2) * LANES + (h % 2) * AUG
    sel = np.zeros((3, N_FF_PAD, D_FOX), np.float32)
    one = np.zeros((1, D_FOX), np.float32)
    for k in range(3):
        sel[k, h, lane + k] = 1.0
        one[0, lane + 3 + k] = 1.0
    return jnp.asarray(sel, BF16), jnp.asarray(one)


def _skip_threshold(gq, gk):
    smax = 1.02 * 8.0 * jnp.max(jnp.abs(gq)) * jnp.max(jnp.abs(gk)) * LOG2E
    return (-(EXP2_UNDERFLOW + 2.0 * smax)).reshape(1).astype(F32)


def _block_starts(ct, batch, seq):
    nh = ct.shape[0]
    ends = ct.reshape(nh, batch, seq // TK, TK)[..., TK - 1]
    starts = jnp.pad(ends, ((0, 0), (0, 0), (1, 0)))
    return starts.transpose(1, 0, 2).reshape(-1)


def _lora_rows(w, start):
    return jnp.pad(w, ((start, N_LORA_PAD - start - w.shape[0]), (0, 0))).astype(BF16)


def kernel(x, p, norm_mix, w_in, mu_shift, w0, w2, a0, a2, g2, k_k, k_a, r_k, gn_w, gn_b, b_f, q_norm, k_norm, w_out, norm_mlp, w_ff1, w_ff2, norm_ple, w_ple_gate, w_ple_proj, norm_final):
    batch, seq, d = x.shape
    depth = w_in.shape[0]
    t = batch * seq
    nh_fox = D_FOX // HEAD_DIM
    tm = min(512, seq)
    sel, one = _bias_lane_tables()
    bd256 = _block_diag_ones(256)
    bd128 = _block_diag_ones(LANES)
    row = lambda a: a.reshape(1, -1)

    h = x.reshape(t, d)
    for i in range(depth):
        order = jnp.argsort(b_f[i])
        z = _in_proj(h, row(norm_mix[i]), _pack_w_in(w_in[i], order), tm=min(1024, seq), tn=512)

        mu = mu_shift[i]
        feats = _rwkv_prep(
            z, seq, row(mu[:3 * D_RWKV]), row(jnp.pad(mu[3 * D_RWKV:], (0, N_LORA_PAD - N_LORA))),
            row(w0[i]), _lora_rows(w2[i], 0), row(a0[i]), _lora_rows(a2[i], 64),
            _lora_rows(g2[i], 128), row(k_k[i]), row(k_a[i]), bd256, tm=min(256, seq))
        y_rw = _rwkv_scan(feats[:6], feats[6], row(gn_w[i]), row(gn_b[i]), row(r_k[i]), bd256,
                          batch, seq, tc=min(1024, seq))

        qt, ka, vt, ct = _fox_prep(
            z, seq, row(jnp.tile(q_norm[i], nh_fox)), row(jnp.tile(k_norm[i], nh_fox)),
            row(jnp.pad(jnp.take(b_f[i], order), (0, N_FF_PAD - nh_fox))), bd256, sel, one)
        y_fox = _fox_attention(_skip_threshold(q_norm[i], k_norm[i]), _block_starts(ct, batch, seq),
                               qt, ka, vt, ct, batch, seq)

        wo = w_out[i].astype(BF16)
        h = _out_proj(h, y_rw.reshape(t, D_RWKV), y_fox.reshape(t, D_FOX), wo[:D_RWKV],
                      _take_heads(wo[D_RWKV:], order, 0), tm=tm)
        h = _mlp(h, row(norm_mlp[i]), w_ff1[i].astype(BF16), w_ff2[i].astype(BF16),
                 tm=min(1024, seq), tf=512)
        h = _ple(h, row(norm_ple[i]), p.reshape(depth, t, -1), i, w_ple_gate[i].astype(BF16),
                 w_ple_proj[i].astype(BF16), row(norm_final), tm=tm, final=i == depth - 1)
    return h.reshape(batch, seq, d)
```

```python
import functools

import numpy as np
import jax
import jax.numpy as jnp
from jax import lax
from jax.experimental import pallas as pl
from jax.experimental.pallas import tpu as pltpu

F32 = jnp.float32
BF16 = jnp.bfloat16

HEAD_DIM = 64
D_RWKV = 1024
D_FOX = 1024
N_LORA = 288
N_LORA_PAD = 384
N_FF_PAD = 128
NORM_EPS = 1e-6
GN_EPS = 64e-5
CHUNK = 64
LANES = 128
TK = 256
AUG = 8
LOG2E = 1.4426950408889634
EXP2_UNDERFLOW = 152.0
UNROLL = 8
LOOKAHEAD = 2
VMEM_LIMIT = 56 * 1024 * 1024


def _dot(a, b):
    return jnp.dot(a, b, preferred_element_type=F32)


def _dot_nt(a, b):
    return lax.dot_general(a, b, (((1,), (1,)), ((), ())), preferred_element_type=F32)


def _dot_tn(a, b):
    return lax.dot_general(a, b, (((0,), (0,)), ((), ())), preferred_element_type=F32)


def _split3(x):
    h1 = x.astype(BF16)
    r1 = x - h1.astype(F32)
    h2 = r1.astype(BF16)
    h3 = (r1 - h2.astype(F32)).astype(BF16)
    return h1, h2, h3


def _split2(x):
    h1 = x.astype(BF16)
    h2 = (x - h1.astype(F32)).astype(BF16)
    return h1, h2


def _group_sum(x, bd):
    w = bd.shape[0]
    hi, lo = _split2(x)
    outs = []
    for c in range(x.shape[1] // w):
        sl = slice(c * w, (c + 1) * w)
        outs.append(_dot(hi[:, sl], bd) + _dot(lo[:, sl], bd))
    return outs[0] if len(outs) == 1 else jnp.concatenate(outs, axis=1)


def _group_sums(xs, bd):
    rows = xs[0].shape[0]
    parts = []
    for x in xs:
        parts.extend(_split2(x))
    out = _dot(jnp.concatenate(parts, axis=0), bd)
    return [out[2 * k * rows:(2 * k + 1) * rows] + out[(2 * k + 1) * rows:(2 * k + 2) * rows]
            for k in range(len(xs))]


def _cumsum_rows(tri, x):
    h1, h2, h3 = _split3(x)
    return _dot(tri, h1) + _dot(tri, h2) + _dot(tri, h3)


def _prefix_sum_rows(x):
    row = lax.broadcasted_iota(jnp.int32, x.shape, 0)
    shift = 1
    while shift < x.shape[0]:
        x = x + jnp.where(row >= shift, pltpu.roll(x, shift, 0), 0.0)
        shift *= 2
    return x


def _sigmoid(x):
    return 1.0 / (1.0 + jnp.exp(-x))


def _softplus(x):
    return jnp.maximum(x, 0.0) + jnp.log1p(jnp.exp(-jnp.abs(x)))


def _rms(x, g):
    ms = jnp.mean(x * x, axis=-1, keepdims=True)
    return x * lax.rsqrt(ms + NORM_EPS) * g


def _params(sem):
    return pltpu.CompilerParams(dimension_semantics=sem, vmem_limit_bytes=VMEM_LIMIT)


def _in_proj_kernel(x_ref, g_ref, w_ref, o_ref, xn_ref):
    @pl.when(pl.program_id(1) == 0)
    def _():
        xn_ref[...] = _rms(x_ref[...], g_ref[...]).astype(BF16)

    o_ref[...] = _dot(xn_ref[...], w_ref[...])


def _in_proj(h, g, w, *, tm, tn):
    t, d = h.shape
    n = w.shape[1]
    return pl.pallas_call(
        _in_proj_kernel,
        out_shape=jax.ShapeDtypeStruct((t, n), F32),
        grid=(t // tm, n // tn),
        in_specs=[
            pl.BlockSpec((tm, d), lambda i, j: (i, 0)),
            pl.BlockSpec((1, d), lambda i, j: (0, 0)),
            pl.BlockSpec((d, tn), lambda i, j: (0, j)),
        ],
        out_specs=pl.BlockSpec((tm, tn), lambda i, j: (i, j)),
        scratch_shapes=[pltpu.VMEM((tm, d), BF16)],
        compiler_params=_params(("parallel", "arbitrary")),
        name="in_proj",
    )(h, g, w)


def _rwkv_prep_kernel(zr_ref, zl_ref, pr_ref, pl_ref, mur_ref, mul_ref, w0_ref, w2_ref, a0_ref,
                      a2_ref, g2_ref, kkw_ref, kaw_ref, bd_ref,
                      r_out, ld_out, kp_out, v_out, kk_out, b_out, g_out, *, blocks_per_seq):
    first = (pl.program_id(0) % blocks_per_seq) == 0

    def shift(z, prev8, mu):
        last = jnp.where(first, 0.0, prev8[7:8, :])
        rolled = pltpu.roll(z, 1, 0)
        row = lax.broadcasted_iota(jnp.int32, z.shape, 0)
        prev = jnp.where(row == 0, last, rolled)
        return z + (prev - z) * mu

    zs = shift(zr_ref[...], pr_ref[...], mur_ref[...])
    zl = shift(zl_ref[...], pl_ref[...], mul_ref[...])
    r = zs[:, :D_RWKV]
    k = zs[:, D_RWKV:2 * D_RWKV]
    v = zs[:, 2 * D_RWKV:]

    lw = w0_ref[...] + _dot(jnp.tanh(zl).astype(BF16), w2_ref[...])
    w_log2 = ((jnp.minimum(lw, 0.0) - 0.5) * LOG2E
              - jnp.log2(1.0 + jnp.exp2(jnp.abs(lw) * -LOG2E)))
    ld = -jnp.exp2(w_log2)
    a = _sigmoid(a0_ref[...] + _dot(zl.astype(BF16), a2_ref[...]))
    g = _dot(_sigmoid(zl).astype(BF16), g2_ref[...])

    kkr = k * kkw_ref[...]
    ss = _group_sum(kkr * kkr, bd_ref[...])
    kk = kkr * lax.rsqrt(jnp.maximum(ss, 1e-24))
    kp = k * (1.0 + (a - 1.0) * kaw_ref[...])

    r_out[...] = r.astype(r_out.dtype)
    ld_out[...] = ld
    kp_out[...] = kp.astype(kp_out.dtype)
    v_out[...] = v.astype(v_out.dtype)
    kk_out[...] = kk.astype(kk_out.dtype)
    b_out[...] = (kk * a).astype(b_out.dtype)
    g_out[...] = g.astype(g_out.dtype)


def _rwkv_prep(z, seq, mu_r, mu_l, w0, w2p, a0, a2p, g2p, k_k, k_a, bd, *, tm):
    t = z.shape[0]
    c_lora = (3 * D_RWKV + 3 * D_FOX) // N_LORA_PAD
    rows8 = tm // 8
    row = lambda i: (0, 0)
    prev = lambda i: (jnp.maximum(i * rows8 - 1, 0), 0)
    out = [jax.ShapeDtypeStruct((t, D_RWKV), F32 if name == "ld" else BF16)
           for name in ("r", "ld", "kp", "v", "kk", "b", "g")]
    return pl.pallas_call(
        functools.partial(_rwkv_prep_kernel, blocks_per_seq=seq // tm),
        out_shape=out,
        grid=(t // tm,),
        in_specs=[
            pl.BlockSpec((tm, 3 * D_RWKV), lambda i: (i, 0)),
            pl.BlockSpec((tm, N_LORA_PAD), lambda i: (i, c_lora)),
            pl.BlockSpec((8, 3 * D_RWKV), prev),
            pl.BlockSpec((8, N_LORA_PAD), lambda i: (jnp.maximum(i * rows8 - 1, 0), c_lora)),
            pl.BlockSpec((1, 3 * D_RWKV), row),
            pl.BlockSpec((1, N_LORA_PAD), row),
            pl.BlockSpec((1, D_RWKV), row),
            pl.BlockSpec((N_LORA_PAD, D_RWKV), row),
            pl.BlockSpec((1, D_RWKV), row),
            pl.BlockSpec((N_LORA_PAD, D_RWKV), row),
            pl.BlockSpec((N_LORA_PAD, D_RWKV), row),
            pl.BlockSpec((1, D_RWKV), row),
            pl.BlockSpec((1, D_RWKV), row),
            pl.BlockSpec(bd.shape, row),
        ],
        out_specs=[pl.BlockSpec((tm, D_RWKV), lambda i: (i, 0))] * 7,
        compiler_params=_params(("parallel",)),
        name="rwkv_prep",
    )(z, z, z, z, mu_r, mu_l, w0, w2p, a0, a2p, g2p, k_k, k_a, bd)


def _stack_heads(q, masks):
    zero = jnp.zeros_like(q)
    return jnp.concatenate([jnp.where(m, q, zero) for m in masks], axis=0).astype(BF16)


def _diag_blocks(full, masks):
    out = full[:HEAD_DIM]
    for g in range(1, len(masks)):
        out = jnp.where(masks[g], full[g * HEAD_DIM:(g + 1) * HEAD_DIM], out)
    return out


def _scan_kernel(r_ref, ld_ref, kp_ref, v_ref, kk_ref, b_ref, g_ref, gnw_ref, gnb_ref, rk_ref,
                 bd_ref, y_ref, s_ref, *, nchunk):
    L = CHUNK
    W = bd_ref.shape[0]
    ngroup = s_ref.shape[1] // W
    lanes = [slice(gp * W, (gp + 1) * W) for gp in range(ngroup)]
    unit_group = [gp for _ in range(nchunk) for gp in range(ngroup)]
    chunks = range(nchunk * ngroup)
    sl = [(slice(c * L, (c + 1) * L), lanes[gp]) for c in range(nchunk) for gp in range(ngroup)]

    @pl.when(pl.program_id(2) == 0)
    def _():
        s_ref[...] = jnp.zeros_like(s_ref)

    lane = lax.broadcasted_iota(jnp.int32, (1, W), 1)
    masks = [lane // HEAD_DIM == g for g in range(W // HEAD_DIM)]
    row = lax.broadcasted_iota(jnp.int32, (L, W), 0)
    col = lax.broadcasted_iota(jnp.int32, (L, W), 1) % HEAD_DIM
    strict = row > col
    incl = row >= col
    eye = (row == col).astype(F32)
    bd = bd_ref[...]
    zero = jnp.zeros((L, W), F32)

    def stack(q):
        return _stack_heads(q, masks)

    def hmm(p, qst):
        return _dot(p.astype(BF16), qst)

    ld = [ld_ref[s] for s in sl]
    cum = [_prefix_sum_rows(x) for x in ld]
    w_last = [jnp.exp(x[L - 1:L, :]) for x in cum]
    w_inv = [jnp.exp(-x) for x in cum]
    r = [r_ref[s].astype(F32) for s in sl]
    kp = [kp_ref[s].astype(F32) for s in sl]
    v = [v_ref[s].astype(F32) for s in sl]
    abar = [-kk_ref[sl[c]].astype(F32) * jnp.exp(cum[c] - ld[c]) for c in chunks]
    rbar = [r[c] * jnp.exp(cum[c]) for c in chunks]
    btil = [b_ref[sl[c]].astype(F32) * w_inv[c] for c in chunks]
    ktil = [kp[c] * w_inv[c] for c in chunks]
    bk = [jnp.concatenate([btil[c] * w_last[c], ktil[c] * w_last[c]], axis=0).astype(BF16)
          for c in chunks]
    x = [jnp.concatenate([abar[c], rbar[c]], axis=0).astype(BF16) for c in chunks]

    a_all = [_dot_nt(x[c], jnp.concatenate([stack(btil[c]), stack(ktil[c])], axis=0))
             for c in chunks]
    n = [jnp.where(strict, a[:L, :W], zero) for a in a_all]
    a_ak = [jnp.where(strict, a[:L, W:], zero) for a in a_all]
    a_rb = [jnp.where(incl, a[L:, :W], zero) for a in a_all]
    a_rk = [jnp.where(incl, a[L:, W:], zero) for a in a_all]

    vst = [stack(x) for x in v]
    av = [hmm(jnp.concatenate([a_ak[c], a_rk[c]], axis=0), vst[c]) for c in chunks]

    tinv = [eye + x for x in n]
    nk = [hmm(x, stack(x)) for x in n]
    for _ in range(4):
        both = [hmm(jnp.concatenate([tinv[c], nk[c]], axis=0), stack(nk[c])) for c in chunks]
        tinv = [tinv[c] + both[c][:L] for c in chunks]
        nk = [x[L:] for x in both]
    tinv = [tinv[c] + hmm(tinv[c], stack(nk[c])) for c in chunks]

    s = [s_ref[:, ln] for ln in lanes]
    ys = [None] * len(chunks)
    for first in range(0, len(chunks), ngroup):
        ids = range(first, first + ngroup)
        xs = [_dot_nt(x[c], stack(s[unit_group[c]])) for c in ids]
        u = [hmm(tinv[c], stack(xs[c - first][:L] + av[c][:L])) for c in ids]
        for c in ids:
            ys[c] = xs[c - first][L:] + hmm(a_rb[c], stack(u[c - first])) + av[c][L:]
        upd = [_diag_blocks(_dot_tn(jnp.concatenate([u[c - first], v[c]], axis=0).astype(BF16),
                                    bk[c]), masks) for c in ids]
        for c in ids:
            s[unit_group[c]] = s[unit_group[c]] * w_last[c] + upd[c - first]
    for gp, ln in enumerate(lanes):
        s_ref[:, ln] = s[gp]

    rk = [rk_ref[:, ln] for ln in lanes]
    sums = _group_sums(ys + [r[c] * kp[c] * rk[unit_group[c]] for c in chunks], bd)
    ds = [ys[c] - sums[c] * (1.0 / HEAD_DIM) for c in chunks]
    var = _group_sums([d * d for d in ds], bd)
    for c in chunks:
        ln = lanes[unit_group[c]]
        yn = ds[c] * lax.rsqrt(var[c] * (1.0 / HEAD_DIM) + GN_EPS) * gnw_ref[:, ln] + gnb_ref[:, ln]
        bonus = sums[len(chunks) + c] * v[c]
        y_ref[sl[c]] = ((yn + bonus) * g_ref[sl[c]].astype(F32)).astype(y_ref.dtype)


def _rwkv_scan(feats, g, gn_w, gn_b, r_k, bd, batch, seq, *, tc, ngroup):
    gw = ngroup * bd.shape[0]
    tok = pl.BlockSpec((None, tc, gw), lambda b, p, c: (b, c, p))
    par = pl.BlockSpec((1, gw), lambda b, p, c: (0, p))
    feats = [f.reshape(batch, seq, D_RWKV) for f in feats]
    return pl.pallas_call(
        functools.partial(_scan_kernel, nchunk=tc // CHUNK),
        out_shape=jax.ShapeDtypeStruct((batch, seq, D_RWKV), BF16),
        grid=(batch, D_RWKV // gw, seq // tc),
        in_specs=[tok] * 7 + [par] * 3 + [pl.BlockSpec(bd.shape, lambda b, p, c: (0, 0))],
        out_specs=tok,
        scratch_shapes=[pltpu.VMEM((HEAD_DIM, gw), F32)],
        compiler_params=_params(("parallel", "parallel", "arbitrary")),
        name="rwkv_scan",
    )(*feats, g.reshape(batch, seq, D_RWKV), gn_w, gn_b, r_k, bd)


def _fox_prep_kernel(q_ref, k_ref, v_ref, ff_ref, gq_ref, gk_ref, bf_ref, bd_ref, sel_ref, one_ref,
                     qt_ref, ka_ref, vt_ref, ct_ref, carry_ref, *, blocks_per_seq):
    @pl.when((pl.program_id(0) % blocks_per_seq) == 0)
    def _():
        carry_ref[...] = jnp.zeros_like(carry_ref)

    bd = bd_ref[...]
    q = q_ref[...]
    k = k_ref[...]
    msq = _group_sum(q * q, bd) * (1.0 / HEAD_DIM)
    msk = _group_sum(k * k, bd) * (1.0 / HEAD_DIM)
    qn = q * lax.rsqrt(msq + NORM_EPS) * gq_ref[...] * (HEAD_DIM ** -0.5 * LOG2E)
    kn = k * lax.rsqrt(msk + NORM_EPS) * gk_ref[...]
    qt_ref[...] = qn.T.astype(BF16)
    vt_ref[...] = v_ref[...].T.astype(BF16)

    tm = q.shape[0]
    log_f = -_softplus(-(ff_ref[...] + bf_ref[...])) * LOG2E
    tri = (lax.broadcasted_iota(jnp.int32, (tm, tm), 0)
           >= lax.broadcasted_iota(jnp.int32, (tm, tm), 1)).astype(BF16)
    c_local = _cumsum_rows(tri, log_f)
    c = c_local + carry_ref[...]
    carry_ref[...] = c[tm - 1:tm, :]
    ct_ref[...] = c.T[:ct_ref.shape[0], :]

    n1, n2, n3 = _split3(-c_local)
    aug = _dot(n1, sel_ref[0]) + _dot(n2, sel_ref[1]) + _dot(n3, sel_ref[2]) + one_ref[...]
    for p in range(D_FOX // LANES):
        src = slice(p * LANES, (p + 1) * LANES)
        ka_ref[:, 2 * p * LANES:(2 * p + 1) * LANES] = kn[:, src].astype(BF16)
        ka_ref[:, (2 * p + 1) * LANES:(2 * p + 2) * LANES] = aug[:, src].astype(BF16)


def _fox_prep(z, seq, gq, gk, bf, bd, sel, one):
    t = z.shape[0]
    tm = TK
    nh = D_FOX // HEAD_DIM
    c_ff = (3 * D_RWKV + 3 * D_FOX + N_LORA_PAD) // N_FF_PAD
    row = lambda i: (0, 0)
    tr = jax.ShapeDtypeStruct((D_FOX, t), BF16)
    return pl.pallas_call(
        functools.partial(_fox_prep_kernel, blocks_per_seq=seq // tm),
        out_shape=[tr, jax.ShapeDtypeStruct((t, 2 * D_FOX), BF16), tr,
                   jax.ShapeDtypeStruct((nh, t), F32)],
        grid=(t // tm,),
        in_specs=[
            pl.BlockSpec((tm, D_FOX), lambda i: (i, 3)),
            pl.BlockSpec((tm, D_FOX), lambda i: (i, 4)),
            pl.BlockSpec((tm, D_FOX), lambda i: (i, 5)),
            pl.BlockSpec((tm, N_FF_PAD), lambda i: (i, c_ff)),
            pl.BlockSpec((1, D_FOX), row),
            pl.BlockSpec((1, D_FOX), row),
            pl.BlockSpec((1, N_FF_PAD), row),
            pl.BlockSpec(bd.shape, row),
            pl.BlockSpec(sel.shape, lambda i: (0, 0, 0)),
            pl.BlockSpec((1, D_FOX), row),
        ],
        out_specs=[pl.BlockSpec((D_FOX, tm), lambda i: (0, i)),
                   pl.BlockSpec((tm, 2 * D_FOX), lambda i: (i, 0)),
                   pl.BlockSpec((D_FOX, tm), lambda i: (0, i)),
                   pl.BlockSpec((nh, tm), lambda i: (0, i))],
        scratch_shapes=[pltpu.VMEM((1, N_FF_PAD), F32)],
        compiler_params=_params(("arbitrary",)),
        name="fox_prep",
    )(z, z, z, z, gq, gk, bf, bd, sel, one)


def _attn_kernel(thr_ref, cref_ref, qt_ref, ka_ref, vt_ref, ct_ref, o_ref, acc_ref, pt_ref,
                 stat_ref, *, nblk):
    def qblock(i, nlive):
        return _attn_qblock(i, nlive, thr_ref, cref_ref, qt_ref, ka_ref, vt_ref, ct_ref, o_ref,
                            acc_ref, pt_ref, stat_ref, nblk=nblk)

    lax.fori_loop(0, nblk, qblock, jnp.int32(0))


def _live_blocks(i, thr, cref_ref, base, nblk):
    ci = [cref_ref[b + i] for b in base]
    lo = [jnp.int32(0)] * len(base)
    hi = [i] * len(base)
    for _ in range(nblk.bit_length()):
        for e, b in enumerate(base):
            mid = (lo[e] + hi[e]) // 2
            live = ci[e] - cref_ref[b + i - mid] >= thr
            is_open = lo[e] < hi[e]
            lo[e], hi[e] = (jnp.where(jnp.logical_and(is_open, live), mid + 1, lo[e]),
                            jnp.where(jnp.logical_and(is_open, jnp.logical_not(live)), mid, hi[e]))
    return functools.reduce(jnp.maximum, lo)


def _attn_qblock(i, nlive, thr_ref, cref_ref, qt_ref, ka_ref, vt_ref, ct_ref, o_ref, acc_ref,
                 pt_ref, stat_ref, *, nblk):
    b = pl.program_id(0)
    p = pl.program_id(1)
    nh = ct_ref.shape[0]
    tq = TK
    qoff = pl.multiple_of(i * tq, tq)

    row_d = lax.broadcasted_iota(jnp.int32, (LANES, 1), 0)
    qt = qt_ref[:, pl.ds(qoff, tq)]
    zq = jnp.zeros_like(qt)
    qte = (jnp.where(row_d < HEAD_DIM, qt, zq), jnp.where(row_d < HEAD_DIM, zq, qt))
    rs = lax.broadcasted_iota(jnp.int32, (16, tq), 0)
    zpad = jnp.zeros((LANES - 16, tq), BF16)
    base = tuple((b * nh + 2 * p + e) * (nblk + 1) for e in range(2))
    ci = tuple(cref_ref[base[e] + i] for e in range(2))

    def rhs_operand(e):
        d = ct_ref[pl.ds(2 * p + e, 1), pl.ds(qoff, tq)] - ci[e]
        d1 = d.astype(BF16).astype(F32)
        r1 = d - d1
        d2 = r1.astype(BF16).astype(F32)
        d3 = r1 - d2
        ones3 = ((rs >= AUG * e) & (rs < AUG * e + 3)).astype(F32)
        strip = jnp.where(rs == AUG * e + 3, d1,
                          jnp.where(rs == AUG * e + 4, d2,
                                    jnp.where(rs == AUG * e + 5, d3, ones3)))
        return jnp.concatenate([qte[e], strip.astype(BF16), zpad], axis=0)

    rhs = tuple(rhs_operand(e) for e in range(2))

    def logits(off, e):
        return _dot(ka_ref[pl.ds(off, TK), :], rhs[e])

    def block_shift(j, e):
        return ci[e] - cref_ref[base[e] + j]

    def col_reduce(x, op):
        return op(op(x.reshape(8, x.shape[0] // 8, x.shape[1]), axis=0), axis=0, keepdims=True)

    def softmax_step(st, e, slot, shift):
        m = stat_ref[e, 0:1, :]
        m_new = jnp.maximum(m, col_reduce(st, jnp.max) + shift)
        alpha = jnp.exp2(m - m_new)
        pt = jnp.exp2(st - (m_new - shift))
        pt_ref[slot, e] = pt.astype(BF16)
        stat_ref[e, 0:1, :] = m_new
        stat_ref[e, 1:2, :] = alpha * stat_ref[e, 1:2, :] + col_reduce(pt, jnp.sum)
        stat_ref[e, 2 + slot:3 + slot, :] = alpha

    def pv_step(off, e, slot):
        vte = vt_ref[pl.ds(HEAD_DIM * e, HEAD_DIM), pl.ds(off, TK)]
        acc_ref[e] = (stat_ref[e, 2 + slot:3 + slot, :] * acc_ref[e]
                      + _dot(vte, pt_ref[slot, e]))

    acc_ref[...] = jnp.zeros_like(acc_ref)
    for e in range(2):
        stat_ref[e, 0:1, :] = jnp.full((1, tq), -jnp.inf, F32)
        stat_ref[e, 1:2, :] = jnp.zeros((1, tq), F32)

    thr = thr_ref[0]

    nlive_next = _live_blocks(i + 1, thr, cref_ref, base, nblk)

    def step(first, nb, diagonal):
        offs = [pl.multiple_of((first - t) * TK, TK) for t in range(nb)]
        sts = [None] * nb
        for t in range(min(LOOKAHEAD, nb)):
            sts[t] = tuple(logits(offs[t], e) for e in range(2))
        if diagonal:
            tt = lax.broadcasted_iota(jnp.int32, (TK, tq), 0)
            rr = lax.broadcasted_iota(jnp.int32, (TK, tq), 1)
            sts[0] = tuple(jnp.where(tt <= rr, st, -jnp.inf) for st in sts[0])
        else:
            for e in range(2):
                pv_step(pl.multiple_of(offs[0] + TK, TK), e, 0)
        for t in range(nb):
            slot = (nb - 1 - t) % 2
            if t + LOOKAHEAD < nb:
                sts[t + LOOKAHEAD] = tuple(logits(offs[t + LOOKAHEAD], e) for e in range(2))
            for e in range(2):
                softmax_step(sts[t][e], e, slot, block_shift(first - t, e))
            if t < nb - 1:
                for e in range(2):
                    pv_step(offs[t], e, slot)

    rem = (nlive + 1) % UNROLL
    nb = 1
    while nb < UNROLL:
        below = rem & (nb - 1)
        pl.when(jnp.logical_and((rem & nb) != 0, below == 0))(
            functools.partial(step, i, nb, True))
        if nb > 1:
            pl.when(jnp.logical_and((rem & nb) != 0, below != 0))(
                functools.partial(step, i - below, nb, False))
        nb *= 2
    pl.when(rem == 0)(functools.partial(step, i, UNROLL, True))
    head = jnp.where(rem == 0, UNROLL, rem)

    def body(k, carry):
        step(i - head - UNROLL * k, UNROLL, False)
        return carry

    lax.fori_loop(0, (nlive + 1 - head) // UNROLL, body, 0)
    last = pl.multiple_of((i - nlive) * TK, TK)
    for e in range(2):
        pv_step(last, e, 0)
    ot = jnp.concatenate([acc_ref[e] / stat_ref[e, 1:2, :] for e in range(2)], axis=0)
    o_ref[pl.ds(qoff, tq), :] = ot.T.astype(o_ref.dtype)
    return nlive_next


def _fox_attention(thr, cref, qt, ka, vt, ct, batch, seq):
    tq = TK
    npair = D_FOX // LANES
    nh = D_FOX // HEAD_DIM
    smem = pl.BlockSpec(memory_space=pltpu.SMEM)
    return pl.pallas_call(
        functools.partial(_attn_kernel, nblk=seq // TK),
        out_shape=jax.ShapeDtypeStruct((batch, seq, D_FOX), BF16),
        grid=(batch, npair),
        in_specs=[
            smem, smem,
            pl.BlockSpec((LANES, seq), lambda b, p: (p, b)),
            pl.BlockSpec((seq, 2 * LANES), lambda b, p: (b, p)),
            pl.BlockSpec((LANES, seq), lambda b, p: (p, b)),
            pl.BlockSpec((nh, seq), lambda b, p: (0, b)),
        ],
        out_specs=pl.BlockSpec((None, seq, LANES), lambda b, p: (b, 0, p)),
        scratch_shapes=[pltpu.VMEM((2, HEAD_DIM, tq), F32), pltpu.VMEM((2, 2, TK, tq), BF16),
                        pltpu.VMEM((2, 8, tq), F32)],
        compiler_params=_params(("parallel", "parallel")),
        name="fox_attn",
    )(thr, cref, qt, ka, vt, ct)


def _out_proj_kernel(h_ref, y1_ref, y2_ref, w1_ref, w2_ref, o_ref):
    o_ref[...] = h_ref[...] + _dot(y1_ref[...], w1_ref[...]) + _dot(y2_ref[...], w2_ref[...])


def _resident(shape):
    return pl.BlockSpec(shape, lambda i: (0,) * len(shape), pipeline_mode=pl.Buffered(1))


def _out_proj(h, y1, y2, w1, w2, *, tm):
    t, d = h.shape
    return pl.pallas_call(
        _out_proj_kernel,
        out_shape=jax.ShapeDtypeStruct((t, d), F32),
        grid=(t // tm,),
        in_specs=[
            pl.BlockSpec((tm, d), lambda i: (i, 0)),
            pl.BlockSpec((tm, y1.shape[1]), lambda i: (i, 0)),
            pl.BlockSpec((tm, y2.shape[1]), lambda i: (i, 0)),
            _resident(w1.shape),
            _resident(w2.shape),
        ],
        out_specs=pl.BlockSpec((tm, d), lambda i: (i, 0)),
        compiler_params=_params(("parallel",)),
        name="out_proj",
    )(h, y1, y2, w1, w2)


def _mlp_kernel(x_ref, g_ref, w1_ref, w2_ref, o_ref, xn_ref):
    @pl.when(pl.program_id(1) == 0)
    def _():
        x = x_ref[...]
        xn_ref[...] = _rms(x, g_ref[...]).astype(BF16)
        o_ref[...] = x

    hid = jnp.maximum(_dot(xn_ref[...], w1_ref[...]), 0.0)
    o_ref[...] += _dot((hid * hid).astype(BF16), w2_ref[...])


def _mlp(h, g, w1, w2, layer, *, tm, tf):
    t, d = h.shape
    f = w1.shape[2]
    return pl.pallas_call(
        _mlp_kernel,
        out_shape=jax.ShapeDtypeStruct((t, d), F32),
        grid=(t // tm, f // tf),
        in_specs=[
            pl.BlockSpec((tm, d), lambda i, j: (i, 0)),
            pl.BlockSpec((1, d), lambda i, j: (0, 0)),
            pl.BlockSpec((None, d, tf), lambda i, j: (layer, 0, j)),
            pl.BlockSpec((None, tf, d), lambda i, j: (layer, j, 0)),
        ],
        out_specs=pl.BlockSpec((tm, d), lambda i, j: (i, 0)),
        scratch_shapes=[pltpu.VMEM((tm, d), BF16)],
        compiler_params=_params(("parallel", "arbitrary")),
        name="mlp",
    )(h, g, w1, w2)


def _ple_kernel(x_ref, g_ref, p_ref, wg_ref, wp_ref, gf_ref, o_ref, *, final):
    x = x_ref[...]
    gate = _sigmoid(_dot(_rms(x, g_ref[...]).astype(BF16), wg_ref[...]))
    h = x + gate * _dot(p_ref[...].astype(BF16), wp_ref[...])
    o_ref[...] = _rms(h, gf_ref[...]) if final else h


def _ple(h, g, p, layer, wg, wp, g_final, *, tm, final):
    t, d = h.shape
    dp = p.shape[2]
    return pl.pallas_call(
        functools.partial(_ple_kernel, final=final),
        out_shape=jax.ShapeDtypeStruct((t, d), F32),
        grid=(t // tm,),
        in_specs=[
            pl.BlockSpec((tm, d), lambda i: (i, 0)),
            _resident((1, d)),
            pl.BlockSpec((None, tm, dp), lambda i: (layer, i, 0)),
            _resident(wg.shape),
            _resident(wp.shape),
            _resident((1, d)),
        ],
        out_specs=pl.BlockSpec((tm, d), lambda i: (i, 0)),
        compiler_params=_params(("parallel",)),
        name="ple",
    )(h, g, p, wg, wp, g_final)


def _block_diag_ones(n):
    idx = np.arange(n) // HEAD_DIM
    return jnp.asarray(idx[:, None] == idx[None, :], BF16)


def _pad_cols(w, n):
    return jnp.pad(w, ((0, 0), (0, n - w.shape[1])))


def _take_heads(w, order, axis):
    shape = w.shape
    split = shape[:axis] + (order.shape[0], shape[axis] // order.shape[0]) + shape[axis + 1:]
    return jnp.take(w.reshape(split), order, axis=axis).reshape(shape)


def _pack_w_in(w, order):
    n_shift = 3 * D_RWKV + N_LORA
    rkv = w[:, :3 * D_RWKV]
    lora = _pad_cols(w[:, 3 * D_RWKV:n_shift], N_LORA_PAD)
    fox = [_take_heads(w[:, n_shift + k * D_FOX:n_shift + (k + 1) * D_FOX], order, 1) for k in range(3)]
    ff = _pad_cols(jnp.take(w[:, n_shift + 3 * D_FOX:], order, axis=1), N_FF_PAD)
    return jnp.concatenate([rkv] + fox + [lora, ff], axis=1).astype(BF16)


def _bias_lane_tables():
    nh = D_FOX // HEAD_DIM
    h = np.arange(nh)
    lane = (h // 2) * LANES + (h % 2) * AUG
    sel = np.zeros((3, N_FF_PAD, D_FOX), np.float32)
    one = np.zeros((1, D_FOX), np.float32)
    for k in range(3):
        sel[k, h, lane + k] = 1.0
        one[0, lane + 3 + k] = 1.0
    return jnp.asarray(sel, BF16), jnp.asarray(one)


def _skip_threshold(gq, gk):
    smax = 1.02 * 8.0 * jnp.max(jnp.abs(gq)) * jnp.max(jnp.abs(gk)) * LOG2E
    return (-(EXP2_UNDERFLOW + 2.0 * smax)).reshape(1).astype(F32)


def _block_starts(ct, batch, seq):
    nh = ct.shape[0]
    ends = ct.reshape(nh, batch, seq // TK, TK)[..., TK - 1]
    starts = jnp.pad(ends, ((0, 0), (0, 0), (1, 0)))
    return starts.transpose(1, 0, 2).reshape(-1)


def _lora_rows(w, start):
    return jnp.pad(w, ((start, N_LORA_PAD - start - w.shape[0]), (0, 0))).astype(BF16)


def kernel(x, p, norm_mix, w_in, mu_shift, w0, w2, a0, a2, g2, k_k, k_a, r_k, gn_w, gn_b, b_f, q_norm, k_norm, w_out, norm_mlp, w_ff1, w_ff2, norm_ple, w_ple_gate, w_ple_proj, norm_final):
    batch, seq, d = x.shape
    depth = w_in.shape[0]
    t = batch * seq
    nh_fox = D_FOX // HEAD_DIM
    tm = min(512, seq)
    sel, one = _bias_lane_tables()
    bd256 = _block_diag_ones(256)
    bd128 = _block_diag_ones(LANES)
    row = lambda a: a.reshape(1, -1)

    w1_all = w_ff1.astype(BF16)
    w2_all = w_ff2.astype(BF16)
    h = x.reshape(t, d)
    for i in range(depth):
        order = jnp.argsort(b_f[i])
        z = _in_proj(h, row(norm_mix[i]), _pack_w_in(w_in[i], order), tm=min(1024, seq), tn=512)

        mu = mu_shift[i]
        feats = _rwkv_prep(
            z, seq, row(mu[:3 * D_RWKV]), row(jnp.pad(mu[3 * D_RWKV:], (0, N_LORA_PAD - N_LORA))),
            row(w0[i]), _lora_rows(w2[i], 0), row(a0[i]), _lora_rows(a2[i], 64),
            _lora_rows(g2[i], 128), row(k_k[i]), row(k_a[i]), bd256, tm=min(256, seq))
        y_rw = _rwkv_scan(feats[:6], feats[6], row(gn_w[i]), row(gn_b[i]), row(r_k[i]), bd256,
                          batch, seq, tc=min(256, seq), ngroup=4)

        qt, ka, vt, ct = _fox_prep(
            z, seq, row(jnp.tile(q_norm[i], nh_fox)), row(jnp.tile(k_norm[i], nh_fox)),
            row(jnp.pad(jnp.take(b_f[i], order), (0, N_FF_PAD - nh_fox))), bd256, sel, one)
        y_fox = _fox_attention(_skip_threshold(q_norm[i], k_norm[i]), _block_starts(ct, batch, seq),
                               qt, ka, vt, ct, batch, seq)

        wo = w_out[i].astype(BF16)
        h = _out_proj(h, y_rw.reshape(t, D_RWKV), y_fox.reshape(t, D_FOX), wo[:D_RWKV],
                      _take_heads(wo[D_RWKV:], order, 0), tm=tm)
        h = _mlp(h, row(norm_mlp[i]), w1_all, w2_all, i, tm=min(1024, seq), tf=512)
        h = _ple(h, row(norm_ple[i]), p.reshape(depth, t, -1), i, w_ple_gate[i].astype(BF16),
                 w_ple_proj[i].astype(BF16), row(norm_final), tm=tm, final=i == depth - 1)
    return h.reshape(batch, seq, d)
```

```python
import functools

import numpy as np
import jax
import jax.numpy as jnp
from jax import lax
from jax.experimental import pallas as pl
from jax.experimental.pallas import tpu as pltpu

F32 = jnp.float32
BF16 = jnp.bfloat16

HEAD_DIM = 64
D_RWKV = 1024
D_FOX = 1024
N_LORA = 288
N_LORA_PAD = 384
N_FF_PAD = 128
N_TAIL = N_LORA_PAD + N_FF_PAD
PREV_ROWS = 16
NORM_EPS = 1e-6
GN_EPS = 64e-5
CHUNK = 64
LANES = 128
TK = 256
AUG = 8
LOG2E = 1.4426950408889634
EXP2_UNDERFLOW = 152.0
UNROLL = 8
LOOKAHEAD = 2
VMEM_LIMIT = 56 * 1024 * 1024


def _dot(a, b):
    return jnp.dot(a, b, preferred_element_type=F32)


def _dot_nt(a, b):
    return lax.dot_general(a, b, (((1,), (1,)), ((), ())), preferred_element_type=F32)


def _dot_tn(a, b):
    return lax.dot_general(a, b, (((0,), (0,)), ((), ())), preferred_element_type=F32)


def _split3(x):
    h1 = x.astype(BF16)
    r1 = x - h1.astype(F32)
    h2 = r1.astype(BF16)
    h3 = (r1 - h2.astype(F32)).astype(BF16)
    return h1, h2, h3


def _split2(x):
    h1 = x.astype(BF16)
    h2 = (x - h1.astype(F32)).astype(BF16)
    return h1, h2


def _group_sum(x, bd):
    w = bd.shape[0]
    hi, lo = _split2(x)
    outs = []
    for c in range(x.shape[1] // w):
        sl = slice(c * w, (c + 1) * w)
        outs.append(_dot(hi[:, sl], bd) + _dot(lo[:, sl], bd))
    return outs[0] if len(outs) == 1 else jnp.concatenate(outs, axis=1)


def _group_sums(xs, bd):
    rows = xs[0].shape[0]
    parts = []
    for x in xs:
        parts.extend(_split2(x))
    out = _dot(jnp.concatenate(parts, axis=0), bd)
    return [out[2 * k * rows:(2 * k + 1) * rows] + out[(2 * k + 1) * rows:(2 * k + 2) * rows]
            for k in range(len(xs))]


def _cumsum_rows(tri, x):
    h1, h2, h3 = _split3(x)
    return _dot(tri, h1) + _dot(tri, h2) + _dot(tri, h3)


def _prefix_sum_rows(x):
    row = lax.broadcasted_iota(jnp.int32, x.shape, 0)
    shift = 1
    while shift < x.shape[0]:
        x = x + jnp.where(row >= shift, pltpu.roll(x, shift, 0), 0.0)
        shift *= 2
    return x


def _sigmoid(x):
    return 1.0 / (1.0 + jnp.exp(-x))


def _softplus(x):
    return jnp.maximum(x, 0.0) + jnp.log1p(jnp.exp(-jnp.abs(x)))


def _rms(x, g):
    ms = jnp.mean(x * x, axis=-1, keepdims=True)
    return x * lax.rsqrt(ms + NORM_EPS) * g


def _params(sem):
    return pltpu.CompilerParams(dimension_semantics=sem, vmem_limit_bytes=VMEM_LIMIT)


def _in_proj_kernel(x_ref, g_ref, w_ref, main_ref, tail_ref, xn_ref, *, n_main):
    j = pl.program_id(1)

    @pl.when(j == 0)
    def _():
        xn_ref[...] = _rms(x_ref[...], g_ref[...]).astype(BF16)

    z = _dot(xn_ref[...], w_ref[...])

    @pl.when(j < n_main)
    def _():
        main_ref[...] = z.astype(main_ref.dtype)

    @pl.when(j == n_main)
    def _():
        tail_ref[...] = z


def _in_proj(h, g, w, *, tm):
    t, d = h.shape
    tn = N_TAIL
    n_main = (w.shape[1] - tn) // tn
    return pl.pallas_call(
        functools.partial(_in_proj_kernel, n_main=n_main),
        out_shape=[jax.ShapeDtypeStruct((t, n_main * tn), BF16), jax.ShapeDtypeStruct((t, tn), F32)],
        grid=(t // tm, n_main + 1),
        in_specs=[
            pl.BlockSpec((tm, d), lambda i, j: (i, 0)),
            pl.BlockSpec((1, d), lambda i, j: (0, 0)),
            pl.BlockSpec((d, tn), lambda i, j: (0, j)),
        ],
        out_specs=[pl.BlockSpec((tm, tn), lambda i, j: (i, jnp.minimum(j, n_main - 1))),
                   pl.BlockSpec((tm, tn), lambda i, j: (i, 0))],
        scratch_shapes=[pltpu.VMEM((tm, d), BF16)],
        compiler_params=_params(("parallel", "arbitrary")),
        name="in_proj",
    )(h, g, w)


def _rwkv_prep_kernel(zr_ref, zl_ref, pr_ref, pl_ref, mur_ref, mul_ref, w0_ref, w2_ref, a0_ref,
                      a2_ref, g2_ref, kkw_ref, kaw_ref, bd_ref,
                      r_out, ld_out, kp_out, v_out, kk_out, b_out, g_out, *, blocks_per_seq):
    first = (pl.program_id(0) % blocks_per_seq) == 0

    def shift(z, before, mu):
        z = z.astype(F32)
        last = jnp.where(first, 0.0, before[PREV_ROWS - 1:PREV_ROWS, :].astype(F32))
        rolled = pltpu.roll(z, 1, 0)
        row = lax.broadcasted_iota(jnp.int32, z.shape, 0)
        prev = jnp.where(row == 0, last, rolled)
        return z + (prev - z) * mu

    zs = shift(zr_ref[...], pr_ref[...], mur_ref[...])
    zl = shift(zl_ref[...], pl_ref[...], mul_ref[...])
    r = zs[:, :D_RWKV]
    k = zs[:, D_RWKV:2 * D_RWKV]
    v = zs[:, 2 * D_RWKV:]

    lw = w0_ref[...] + _dot(jnp.tanh(zl).astype(BF16), w2_ref[...])
    w_log2 = ((jnp.minimum(lw, 0.0) - 0.5) * LOG2E
              - jnp.log2(1.0 + jnp.exp2(jnp.abs(lw) * -LOG2E)))
    ld = -jnp.exp2(w_log2)
    a = _sigmoid(a0_ref[...] + _dot(zl.astype(BF16), a2_ref[...]))
    g = _dot(_sigmoid(zl).astype(BF16), g2_ref[...])

    kkr = k * kkw_ref[...]
    ss = _group_sum(kkr * kkr, bd_ref[...])
    kk = kkr * lax.rsqrt(jnp.maximum(ss, 1e-24))
    kp = k * (1.0 + (a - 1.0) * kaw_ref[...])

    r_out[...] = r.astype(r_out.dtype)
    ld_out[...] = ld
    kp_out[...] = kp.astype(kp_out.dtype)
    v_out[...] = v.astype(v_out.dtype)
    kk_out[...] = kk.astype(kk_out.dtype)
    b_out[...] = (kk * a).astype(b_out.dtype)
    g_out[...] = g.astype(g_out.dtype)


def _rwkv_prep(z, zt, seq, mu_r, mu_l, w0, w2p, a0, a2p, g2p, k_k, k_a, bd, *, tm):
    t = z.shape[0]
    row = lambda i: (0, 0)
    prev = lambda i: (jnp.maximum(i * (tm // PREV_ROWS) - 1, 0), 0)
    out = [jax.ShapeDtypeStruct((t, D_RWKV), F32 if name == "ld" else BF16)
           for name in ("r", "ld", "kp", "v", "kk", "b", "g")]
    return pl.pallas_call(
        functools.partial(_rwkv_prep_kernel, blocks_per_seq=seq // tm),
        out_shape=out,
        grid=(t // tm,),
        in_specs=[
            pl.BlockSpec((tm, 3 * D_RWKV), lambda i: (i, 0)),
            pl.BlockSpec((tm, N_LORA_PAD), lambda i: (i, 0)),
            pl.BlockSpec((PREV_ROWS, 3 * D_RWKV), prev),
            pl.BlockSpec((PREV_ROWS, N_LORA_PAD), prev),
            pl.BlockSpec((1, 3 * D_RWKV), row),
            pl.BlockSpec((1, N_LORA_PAD), row),
            pl.BlockSpec((1, D_RWKV), row),
            pl.BlockSpec((N_LORA_PAD, D_RWKV), row),
            pl.BlockSpec((1, D_RWKV), row),
            pl.BlockSpec((N_LORA_PAD, D_RWKV), row),
            pl.BlockSpec((N_LORA_PAD, D_RWKV), row),
            pl.BlockSpec((1, D_RWKV), row),
            pl.BlockSpec((1, D_RWKV), row),
            pl.BlockSpec(bd.shape, row),
        ],
        out_specs=[pl.BlockSpec((tm, D_RWKV), lambda i: (i, 0))] * 7,
        compiler_params=_params(("parallel",)),
        name="rwkv_prep",
    )(z, zt, z, zt, mu_r, mu_l, w0, w2p, a0, a2p, g2p, k_k, k_a, bd)


def _stack_heads(q, masks):
    zero = jnp.zeros_like(q)
    return jnp.concatenate([jnp.where(m, q, zero) for m in masks], axis=0).astype(BF16)


def _diag_blocks(full, masks):
    out = full[:HEAD_DIM]
    for g in range(1, len(masks)):
        out = jnp.where(masks[g], full[g * HEAD_DIM:(g + 1) * HEAD_DIM], out)
    return out


def _scan_kernel(r_ref, ld_ref, kp_ref, v_ref, kk_ref, b_ref, g_ref, gnw_ref, gnb_ref, rk_ref,
                 bd_ref, y_ref, s_ref, *, nchunk):
    L = CHUNK
    W = bd_ref.shape[0]
    ngroup = s_ref.shape[1] // W
    lanes = [slice(gp * W, (gp + 1) * W) for gp in range(ngroup)]
    unit_group = [gp for _ in range(nchunk) for gp in range(ngroup)]
    chunks = range(nchunk * ngroup)
    sl = [(slice(c * L, (c + 1) * L), lanes[gp]) for c in range(nchunk) for gp in range(ngroup)]

    @pl.when(pl.program_id(2) == 0)
    def _():
        s_ref[...] = jnp.zeros_like(s_ref)

    lane = lax.broadcasted_iota(jnp.int32, (1, W), 1)
    masks = [lane // HEAD_DIM == g for g in range(W // HEAD_DIM)]
    row = lax.broadcasted_iota(jnp.int32, (L, W), 0)
    col = lax.broadcasted_iota(jnp.int32, (L, W), 1) % HEAD_DIM
    strict = row > col
    incl = row >= col
    eye = (row == col).astype(F32)
    bd = bd_ref[...]
    zero = jnp.zeros((L, W), F32)

    def stack(q):
        return _stack_heads(q, masks)

    def hmm(p, qst):
        return _dot(p.astype(BF16), qst)

    ld = [ld_ref[s] for s in sl]
    cum = [_prefix_sum_rows(x) for x in ld]
    w_last = [jnp.exp(x[L - 1:L, :]) for x in cum]
    w_inv = [jnp.exp(-x) for x in cum]
    r = [r_ref[s].astype(F32) for s in sl]
    kp = [kp_ref[s].astype(F32) for s in sl]
    v = [v_ref[s].astype(F32) for s in sl]
    abar = [-kk_ref[sl[c]].astype(F32) * jnp.exp(cum[c] - ld[c]) for c in chunks]
    rbar = [r[c] * jnp.exp(cum[c]) for c in chunks]
    btil = [b_ref[sl[c]].astype(F32) * w_inv[c] for c in chunks]
    ktil = [kp[c] * w_inv[c] for c in chunks]
    bk = [jnp.concatenate([btil[c] * w_last[c], ktil[c] * w_last[c]], axis=0).astype(BF16)
          for c in chunks]
    x = [jnp.concatenate([abar[c], rbar[c]], axis=0).astype(BF16) for c in chunks]

    a_all = [_dot_nt(x[c], jnp.concatenate([stack(btil[c]), stack(ktil[c])], axis=0))
             for c in chunks]
    n = [jnp.where(strict, a[:L, :W], zero) for a in a_all]
    a_ak = [jnp.where(strict, a[:L, W:], zero) for a in a_all]
    a_rb = [jnp.where(incl, a[L:, :W], zero) for a in a_all]
    a_rk = [jnp.where(incl, a[L:, W:], zero) for a in a_all]

    vst = [stack(x) for x in v]
    av = [hmm(jnp.concatenate([a_ak[c], a_rk[c]], axis=0), vst[c]) for c in chunks]

    tinv = [eye + x for x in n]
    nk = [hmm(x, stack(x)) for x in n]
    for _ in range(4):
        both = [hmm(jnp.concatenate([tinv[c], nk[c]], axis=0), stack(nk[c])) for c in chunks]
        tinv = [tinv[c] + both[c][:L] for c in chunks]
        nk = [x[L:] for x in both]
    tinv = [tinv[c] + hmm(tinv[c], stack(nk[c])) for c in chunks]

    s = [s_ref[:, ln] for ln in lanes]
    ys = [None] * len(chunks)
    for first in range(0, len(chunks), ngroup):
        ids = range(first, first + ngroup)
        xs = [_dot_nt(x[c], stack(s[unit_group[c]])) for c in ids]
        u = [hmm(tinv[c], stack(xs[c - first][:L] + av[c][:L])) for c in ids]
        for c in ids:
            ys[c] = xs[c - first][L:] + hmm(a_rb[c], stack(u[c - first])) + av[c][L:]
        upd = [_diag_blocks(_dot_tn(jnp.concatenate([u[c - first], v[c]], axis=0).astype(BF16),
                                    bk[c]), masks) for c in ids]
        for c in ids:
            s[unit_group[c]] = s[unit_group[c]] * w_last[c] + upd[c - first]
    for gp, ln in enumerate(lanes):
        s_ref[:, ln] = s[gp]

    rk = [rk_ref[:, ln] for ln in lanes]
    sums = _group_sums(ys + [r[c] * kp[c] * rk[unit_group[c]] for c in chunks], bd)
    ds = [ys[c] - sums[c] * (1.0 / HEAD_DIM) for c in chunks]
    var = _group_sums([d * d for d in ds], bd)
    for c in chunks:
        ln = lanes[unit_group[c]]
        yn = ds[c] * lax.rsqrt(var[c] * (1.0 / HEAD_DIM) + GN_EPS) * gnw_ref[:, ln] + gnb_ref[:, ln]
        bonus = sums[len(chunks) + c] * v[c]
        y_ref[sl[c]] = ((yn + bonus) * g_ref[sl[c]].astype(F32)).astype(y_ref.dtype)


def _rwkv_scan(feats, g, gn_w, gn_b, r_k, bd, batch, seq, *, tc, ngroup):
    gw = ngroup * bd.shape[0]
    tok = pl.BlockSpec((None, tc, gw), lambda b, p, c: (b, c, p))
    par = pl.BlockSpec((1, gw), lambda b, p, c: (0, p))
    feats = [f.reshape(batch, seq, D_RWKV) for f in feats]
    return pl.pallas_call(
        functools.partial(_scan_kernel, nchunk=tc // CHUNK),
        out_shape=jax.ShapeDtypeStruct((batch, seq, D_RWKV), BF16),
        grid=(batch, D_RWKV // gw, seq // tc),
        in_specs=[tok] * 7 + [par] * 3 + [pl.BlockSpec(bd.shape, lambda b, p, c: (0, 0))],
        out_specs=tok,
        scratch_shapes=[pltpu.VMEM((HEAD_DIM, gw), F32)],
        compiler_params=_params(("parallel", "parallel", "arbitrary")),
        name="rwkv_scan",
    )(*feats, g.reshape(batch, seq, D_RWKV), gn_w, gn_b, r_k, bd)


def _fox_prep_kernel(q_ref, k_ref, v_ref, ff_ref, gq_ref, gk_ref, bf_ref, bd_ref, sel_ref, one_ref,
                     qt_ref, ka_ref, vt_ref, ct_ref, carry_ref, *, blocks_per_seq):
    @pl.when((pl.program_id(0) % blocks_per_seq) == 0)
    def _():
        carry_ref[...] = jnp.zeros_like(carry_ref)

    bd = bd_ref[...]
    q = q_ref[...].astype(F32)
    k = k_ref[...].astype(F32)
    msq = _group_sum(q * q, bd) * (1.0 / HEAD_DIM)
    msk = _group_sum(k * k, bd) * (1.0 / HEAD_DIM)
    qn = q * lax.rsqrt(msq + NORM_EPS) * gq_ref[...] * (HEAD_DIM ** -0.5 * LOG2E)
    kn = k * lax.rsqrt(msk + NORM_EPS) * gk_ref[...]
    qt_ref[...] = qn.T.astype(BF16)
    vt_ref[...] = v_ref[...].astype(F32).T.astype(BF16)

    tm = q.shape[0]
    log_f = -_softplus(-(ff_ref[...] + bf_ref[...])) * LOG2E
    tri = (lax.broadcasted_iota(jnp.int32, (tm, tm), 0)
           >= lax.broadcasted_iota(jnp.int32, (tm, tm), 1)).astype(BF16)
    c_local = _cumsum_rows(tri, log_f)
    c = c_local + carry_ref[...]
    carry_ref[...] = c[tm - 1:tm, :]
    ct_ref[...] = c.T[:ct_ref.shape[0], :]

    n1, n2, n3 = _split3(-c_local)
    aug = _dot(n1, sel_ref[0]) + _dot(n2, sel_ref[1]) + _dot(n3, sel_ref[2]) + one_ref[...]
    for p in range(D_FOX // LANES):
        src = slice(p * LANES, (p + 1) * LANES)
        ka_ref[:, 2 * p * LANES:(2 * p + 1) * LANES] = kn[:, src].astype(BF16)
        ka_ref[:, (2 * p + 1) * LANES:(2 * p + 2) * LANES] = aug[:, src].astype(BF16)


def _fox_prep(z, zt, seq, gq, gk, bf, bd, sel, one):
    t = z.shape[0]
    tm = TK
    nh = D_FOX // HEAD_DIM
    c_ff = N_LORA_PAD // N_FF_PAD
    row = lambda i: (0, 0)
    tr = jax.ShapeDtypeStruct((D_FOX, t), BF16)
    return pl.pallas_call(
        functools.partial(_fox_prep_kernel, blocks_per_seq=seq // tm),
        out_shape=[tr, jax.ShapeDtypeStruct((t, 2 * D_FOX), BF16), tr,
                   jax.ShapeDtypeStruct((nh, t), F32)],
        grid=(t // tm,),
        in_specs=[
            pl.BlockSpec((tm, D_FOX), lambda i: (i, 3)),
            pl.BlockSpec((tm, D_FOX), lambda i: (i, 4)),
            pl.BlockSpec((tm, D_FOX), lambda i: (i, 5)),
            pl.BlockSpec((tm, N_FF_PAD), lambda i: (i, c_ff)),
            pl.BlockSpec((1, D_FOX), row),
            pl.BlockSpec((1, D_FOX), row),
            pl.BlockSpec((1, N_FF_PAD), row),
            pl.BlockSpec(bd.shape, row),
            pl.BlockSpec(sel.shape, lambda i: (0, 0, 0)),
            pl.BlockSpec((1, D_FOX), row),
        ],
        out_specs=[pl.BlockSpec((D_FOX, tm), lambda i: (0, i)),
                   pl.BlockSpec((tm, 2 * D_FOX), lambda i: (i, 0)),
                   pl.BlockSpec((D_FOX, tm), lambda i: (0, i)),
                   pl.BlockSpec((nh, tm), lambda i: (0, i))],
        scratch_shapes=[pltpu.VMEM((1, N_FF_PAD), F32)],
        compiler_params=_params(("arbitrary",)),
        name="fox_prep",
    )(z, z, z, zt, gq, gk, bf, bd, sel, one)


def _attn_kernel(thr_ref, cref_ref, qt_ref, ka_ref, vt_ref, ct_ref, o_ref, acc_ref, pt_ref,
                 stat_ref, *, nblk):
    def qblock(i, nlive):
        return _attn_qblock(i, nlive, thr_ref, cref_ref, qt_ref, ka_ref, vt_ref, ct_ref, o_ref,
                            acc_ref, pt_ref, stat_ref, nblk=nblk)

    lax.fori_loop(0, nblk, qblock, jnp.int32(0))


def _live_blocks(i, thr, cref_ref, base, nblk):
    ci = [cref_ref[b + i] for b in base]
    lo = [jnp.int32(0)] * len(base)
    hi = [i] * len(base)
    for _ in range(nblk.bit_length()):
        for e, b in enumerate(base):
            mid = (lo[e] + hi[e]) // 2
            live = ci[e] - cref_ref[b + i - mid] >= thr
            is_open = lo[e] < hi[e]
            lo[e], hi[e] = (jnp.where(jnp.logical_and(is_open, live), mid + 1, lo[e]),
                            jnp.where(jnp.logical_and(is_open, jnp.logical_not(live)), mid, hi[e]))
    return functools.reduce(jnp.maximum, lo)


def _attn_qblock(i, nlive, thr_ref, cref_ref, qt_ref, ka_ref, vt_ref, ct_ref, o_ref, acc_ref,
                 pt_ref, stat_ref, *, nblk):
    b = pl.program_id(0)
    p = pl.program_id(1)
    nh = ct_ref.shape[0]
    tq = TK
    qoff = pl.multiple_of(i * tq, tq)

    row_d = lax.broadcasted_iota(jnp.int32, (LANES, 1), 0)
    qt = qt_ref[:, pl.ds(qoff, tq)]
    zq = jnp.zeros_like(qt)
    qte = (jnp.where(row_d < HEAD_DIM, qt, zq), jnp.where(row_d < HEAD_DIM, zq, qt))
    rs = lax.broadcasted_iota(jnp.int32, (16, tq), 0)
    zpad = jnp.zeros((LANES - 16, tq), BF16)
    base = tuple((b * nh + 2 * p + e) * (nblk + 1) for e in range(2))
    ci = tuple(cref_ref[base[e] + i] for e in range(2))

    def rhs_operand(e):
        d = ct_ref[pl.ds(2 * p + e, 1), pl.ds(qoff, tq)] - ci[e]
        d1 = d.astype(BF16).astype(F32)
        r1 = d - d1
        d2 = r1.astype(BF16).astype(F32)
        d3 = r1 - d2
        ones3 = ((rs >= AUG * e) & (rs < AUG * e + 3)).astype(F32)
        strip = jnp.where(rs == AUG * e + 3, d1,
                          jnp.where(rs == AUG * e + 4, d2,
                                    jnp.where(rs == AUG * e + 5, d3, ones3)))
        return jnp.concatenate([qte[e], strip.astype(BF16), zpad], axis=0)

    rhs = tuple(rhs_operand(e) for e in range(2))

    def logits(off, e):
        return _dot(ka_ref[pl.ds(off, TK), :], rhs[e])

    def block_shift(j, e):
        return ci[e] - cref_ref[base[e] + j]

    def col_reduce(x, op):
        return op(op(x.reshape(8, x.shape[0] // 8, x.shape[1]), axis=0), axis=0, keepdims=True)

    def softmax_step(st, e, slot, shift):
        m = stat_ref[e, 0:1, :]
        m_new = jnp.maximum(m, col_reduce(st, jnp.max) + shift)
        alpha = jnp.exp2(m - m_new)
        pt = jnp.exp2(st - (m_new - shift))
        pt_ref[slot, e] = pt.astype(BF16)
        stat_ref[e, 0:1, :] = m_new
        stat_ref[e, 1:2, :] = alpha * stat_ref[e, 1:2, :] + col_reduce(pt, jnp.sum)
        stat_ref[e, 2 + slot:3 + slot, :] = alpha

    def pv_step(off, e, slot):
        vte = vt_ref[pl.ds(HEAD_DIM * e, HEAD_DIM), pl.ds(off, TK)]
        acc_ref[e] = (stat_ref[e, 2 + slot:3 + slot, :] * acc_ref[e]
                      + _dot(vte, pt_ref[slot, e]))

    acc_ref[...] = jnp.zeros_like(acc_ref)
    for e in range(2):
        stat_ref[e, 0:1, :] = jnp.full((1, tq), -jnp.inf, F32)
        stat_ref[e, 1:2, :] = jnp.zeros((1, tq), F32)

    thr = thr_ref[0]

    nlive_next = _live_blocks(i + 1, thr, cref_ref, base, nblk)

    def step(first, nb, diagonal):
        offs = [pl.multiple_of((first - t) * TK, TK) for t in range(nb)]
        sts = [None] * nb
        for t in range(min(LOOKAHEAD, nb)):
            sts[t] = tuple(logits(offs[t], e) for e in range(2))
        if diagonal:
            tt = lax.broadcasted_iota(jnp.int32, (TK, tq), 0)
            rr = lax.broadcasted_iota(jnp.int32, (TK, tq), 1)
            sts[0] = tuple(jnp.where(tt <= rr, st, -jnp.inf) for st in sts[0])
        else:
            for e in range(2):
                pv_step(pl.multiple_of(offs[0] + TK, TK), e, 0)
        for t in range(nb):
            slot = (nb - 1 - t) % 2
            if t + LOOKAHEAD < nb:
                sts[t + LOOKAHEAD] = tuple(logits(offs[t + LOOKAHEAD], e) for e in range(2))
            for e in range(2):
                softmax_step(sts[t][e], e, slot, block_shift(first - t, e))
            if t < nb - 1:
                for e in range(2):
                    pv_step(offs[t], e, slot)

    rem = (nlive + 1) % UNROLL
    nb = 1
    while nb < UNROLL:
        below = rem & (nb - 1)
        pl.when(jnp.logical_and((rem & nb) != 0, below == 0))(
            functools.partial(step, i, nb, True))
        if nb > 1:
            pl.when(jnp.logical_and((rem & nb) != 0, below != 0))(
                functools.partial(step, i - below, nb, False))
        nb *= 2
    pl.when(rem == 0)(functools.partial(step, i, UNROLL, True))
    head = jnp.where(rem == 0, UNROLL, rem)

    def body(k, carry):
        step(i - head - UNROLL * k, UNROLL, False)
        return carry

    lax.fori_loop(0, (nlive + 1 - head) // UNROLL, body, 0)
    last = pl.multiple_of((i - nlive) * TK, TK)
    for e in range(2):
        pv_step(last, e, 0)
    ot = jnp.concatenate([acc_ref[e] / stat_ref[e, 1:2, :] for e in range(2)], axis=0)
    o_ref[pl.ds(qoff, tq), :] = ot.T.astype(o_ref.dtype)
    return nlive_next


def _fox_attention(thr, cref, qt, ka, vt, ct, batch, seq):
    tq = TK
    npair = D_FOX // LANES
    nh = D_FOX // HEAD_DIM
    smem = pl.BlockSpec(memory_space=pltpu.SMEM)
    return pl.pallas_call(
        functools.partial(_attn_kernel, nblk=seq // TK),
        out_shape=jax.ShapeDtypeStruct((batch, seq, D_FOX), BF16),
        grid=(batch, npair),
        in_specs=[
            smem, smem,
            pl.BlockSpec((LANES, seq), lambda b, p: (p, b)),
            pl.BlockSpec((seq, 2 * LANES), lambda b, p: (b, p)),
            pl.BlockSpec((LANES, seq), lambda b, p: (p, b)),
            pl.BlockSpec((nh, seq), lambda b, p: (0, b)),
        ],
        out_specs=pl.BlockSpec((None, seq, LANES), lambda b, p: (b, 0, p)),
        scratch_shapes=[pltpu.VMEM((2, HEAD_DIM, tq), F32), pltpu.VMEM((2, 2, TK, tq), BF16),
                        pltpu.VMEM((2, 8, tq), F32)],
        compiler_params=_params(("parallel", "parallel")),
        name="fox_attn",
    )(thr, cref, qt, ka, vt, ct)


def _out_proj_kernel(h_ref, y1_ref, y2_ref, w1_ref, w2_ref, o_ref):
    o_ref[...] = h_ref[...] + _dot(y1_ref[...], w1_ref[...]) + _dot(y2_ref[...], w2_ref[...])


def _resident(shape):
    return pl.BlockSpec(shape, lambda i: (0,) * len(shape), pipeline_mode=pl.Buffered(1))


def _out_proj(h, y1, y2, w1, w2, *, tm):
    t, d = h.shape
    return pl.pallas_call(
        _out_proj_kernel,
        out_shape=jax.ShapeDtypeStruct((t, d), F32),
        grid=(t // tm,),
        in_specs=[
            pl.BlockSpec((tm, d), lambda i: (i, 0)),
            pl.BlockSpec((tm, y1.shape[1]), lambda i: (i, 0)),
            pl.BlockSpec((tm, y2.shape[1]), lambda i: (i, 0)),
            _resident(w1.shape),
            _resident(w2.shape),
        ],
        out_specs=pl.BlockSpec((tm, d), lambda i: (i, 0)),
        compiler_params=_params(("parallel",)),
        name="out_proj",
    )(h, y1, y2, w1, w2)


def _mlp_kernel(x_ref, g_ref, w1_ref, w2_ref, o_ref, xn_ref):
    @pl.when(pl.program_id(1) == 0)
    def _():
        x = x_ref[...]
        xn_ref[...] = _rms(x, g_ref[...]).astype(BF16)
        o_ref[...] = x

    hid = jnp.maximum(_dot(xn_ref[...], w1_ref[...]), 0.0)
    o_ref[...] += _dot((hid * hid).astype(BF16), w2_ref[...])


def _mlp(h, g, w1, w2, layer, *, tm, tf):
    t, d = h.shape
    f = w1.shape[2]
    return pl.pallas_call(
        _mlp_kernel,
        out_shape=jax.ShapeDtypeStruct((t, d), F32),
        grid=(t // tm, f // tf),
        in_specs=[
            pl.BlockSpec((tm, d), lambda i, j: (i, 0)),
            pl.BlockSpec((1, d), lambda i, j: (0, 0)),
            pl.BlockSpec((None, d, tf), lambda i, j: (layer, 0, j)),
            pl.BlockSpec((None, tf, d), lambda i, j: (layer, j, 0)),
        ],
        out_specs=pl.BlockSpec((tm, d), lambda i, j: (i, 0)),
        scratch_shapes=[pltpu.VMEM((tm, d), BF16)],
        compiler_params=_params(("parallel", "arbitrary")),
        name="mlp",
    )(h, g, w1, w2)


def _ple_kernel(x_ref, g_ref, p_ref, wg_ref, wp_ref, gf_ref, o_ref, *, final):
    x = x_ref[...]
    gate = _sigmoid(_dot(_rms(x, g_ref[...]).astype(BF16), wg_ref[...]))
    h = x + gate * _dot(p_ref[...].astype(BF16), wp_ref[...])
    o_ref[...] = _rms(h, gf_ref[...]) if final else h


def _ple(h, g, p, layer, wg, wp, g_final, *, tm, final):
    t, d = h.shape
    dp = p.shape[2]
    return pl.pallas_call(
        functools.partial(_ple_kernel, final=final),
        out_shape=jax.ShapeDtypeStruct((t, d), F32),
        grid=(t // tm,),
        in_specs=[
            pl.BlockSpec((tm, d), lambda i: (i, 0)),
            _resident((1, d)),
            pl.BlockSpec((None, tm, dp), lambda i: (layer, i, 0)),
            _resident(wg.shape),
            _resident(wp.shape),
            _resident((1, d)),
        ],
        out_specs=pl.BlockSpec((tm, d), lambda i: (i, 0)),
        compiler_params=_params(("parallel",)),
        name="ple",
    )(h, g, p, wg, wp, g_final)


def _block_diag_ones(n):
    idx = np.arange(n) // HEAD_DIM
    return jnp.asarray(idx[:, None] == idx[None, :], BF16)


def _pad_cols(w, n):
    return jnp.pad(w, ((0, 0), (0, n - w.shape[1])))


def _take_heads(w, order, axis):
    shape = w.shape
    split = shape[:axis] + (order.shape[0], shape[axis] // order.shape[0]) + shape[axis + 1:]
    return jnp.take(w.reshape(split), order, axis=axis).reshape(shape)


def _pack_w_in(w, order):
    n_shift = 3 * D_RWKV + N_LORA
    rkv = w[:, :3 * D_RWKV]
    lora = _pad_cols(w[:, 3 * D_RWKV:n_shift], N_LORA_PAD)
    fox = [_take_heads(w[:, n_shift + k * D_FOX:n_shift + (k + 1) * D_FOX], order, 1) for k in range(3)]
    ff = _pad_cols(jnp.take(w[:, n_shift + 3 * D_FOX:], order, axis=1), N_FF_PAD)
    return jnp.concatenate([rkv] + fox + [lora, ff], axis=1).astype(BF16)


def _bias_lane_tables():
    nh = D_FOX // HEAD_DIM
    h = np.arange(nh)
    lane = (h // 2) * LANES + (h % 2) * AUG
    sel = np.zeros((3, N_FF_PAD, D_FOX), np.float32)
    one = np.zeros((1, D_FOX), np.float32)
    for k in range(3):
        sel[k, h, lane + k] = 1.0
        one[0, lane + 3 + k] = 1.0
    return jnp.asarray(sel, BF16), jnp.asarray(one)


def _skip_threshold(gq, gk):
    smax = 1.02 * 8.0 * jnp.max(jnp.abs(gq)) * jnp.max(jnp.abs(gk)) * LOG2E
    return (-(EXP2_UNDERFLOW + 2.0 * smax)).reshape(1).astype(F32)


def _block_starts(ct, batch, seq):
    nh = ct.shape[0]
    ends = ct.reshape(nh, batch, seq // TK, TK)[..., TK - 1]
    starts = jnp.pad(ends, ((0, 0), (0, 0), (1, 0)))
    return starts.transpose(1, 0, 2).reshape(-1)


def _lora_rows(w, start):
    return jnp.pad(w, ((start, N_LORA_PAD - start - w.shape[0]), (0, 0))).astype(BF16)


def kernel(x, p, norm_mix, w_in, mu_shift, w0, w2, a0, a2, g2, k_k, k_a, r_k, gn_w, gn_b, b_f, q_norm, k_norm, w_out, norm_mlp, w_ff1, w_ff2, norm_ple, w_ple_gate, w_ple_proj, norm_final):
    batch, seq, d = x.shape
    depth = w_in.shape[0]
    t = batch * seq
    nh_fox = D_FOX // HEAD_DIM
    tm = min(512, seq)
    sel, one = _bias_lane_tables()
    bd256 = _block_diag_ones(256)
    row = lambda a: a.reshape(1, -1)

    w1_all = w_ff1.astype(BF16)
    w2_all = w_ff2.astype(BF16)
    h = x.reshape(t, d)
    for i in range(depth):
        order = jnp.argsort(b_f[i])
        z, zt = _in_proj(h, row(norm_mix[i]), _pack_w_in(w_in[i], order), tm=min(1024, seq))

        mu = mu_shift[i]
        feats = _rwkv_prep(
            z, zt, seq, row(mu[:3 * D_RWKV]), row(jnp.pad(mu[3 * D_RWKV:], (0, N_LORA_PAD - N_LORA))),
            row(w0[i]), _lora_rows(w2[i], 0), row(a0[i]), _lora_rows(a2[i], 64),
            _lora_rows(g2[i], 128), row(k_k[i]), row(k_a[i]), bd256, tm=min(256, seq))
        y_rw = _rwkv_scan(feats[:6], feats[6], row(gn_w[i]), row(gn_b[i]), row(r_k[i]), bd256,
                          batch, seq, tc=min(256, seq), ngroup=4)

        qt, ka, vt, ct = _fox_prep(
            z, zt, seq, row(jnp.tile(q_norm[i], nh_fox)), row(jnp.tile(k_norm[i], nh_fox)),
            row(jnp.pad(jnp.take(b_f[i], order), (0, N_FF_PAD - nh_fox))), bd256, sel, one)
        y_fox = _fox_attention(_skip_threshold(q_norm[i], k_norm[i]), _block_starts(ct, batch, seq),
                               qt, ka, vt, ct, batch, seq)

        wo = w_out[i].astype(BF16)
        h = _out_proj(h, y_rw.reshape(t, D_RWKV), y_fox.reshape(t, D_FOX), wo[:D_RWKV],
                      _take_heads(wo[D_RWKV:], order, 0), tm=tm)
        h = _mlp(h, row(norm_mlp[i]), w1_all, w2_all, i, tm=min(1024, seq), tf=512)
        h = _ple(h, row(norm_ple[i]), p.reshape(depth, t, -1), i, w_ple_gate[i].astype(BF16),
                 w_ple_proj[i].astype(BF16), row(norm_final), tm=tm, final=i == depth - 1)
    return h.reshape(batch, seq, d)
```

```python
import functools

import numpy as np
import jax
import jax.numpy as jnp
from jax import lax
from jax.experimental import pallas as pl
from jax.experimental.pallas import tpu as pltpu

F32 = jnp.float32
BF16 = jnp.bfloat16

HEAD_DIM = 64
D_RWKV = 1024
D_FOX = 1024
N_LORA = 288
N_LORA_PAD = 384
N_FF_PAD = 128
NORM_EPS = 1e-6
GN_EPS = 64e-5
CHUNK = 64
LANES = 128
TK = 256
AUG = 8
LOG2E = 1.4426950408889634
EXP2_UNDERFLOW = 152.0
UNROLL = 8
LOOKAHEAD = 2
VMEM_LIMIT = 56 * 1024 * 1024


def _dot(a, b):
    return jnp.dot(a, b, preferred_element_type=F32)


def _dot_nt(a, b):
    return lax.dot_general(a, b, (((1,), (1,)), ((), ())), preferred_element_type=F32)


def _dot_tn(a, b):
    return lax.dot_general(a, b, (((0,), (0,)), ((), ())), preferred_element_type=F32)


def _split3(x):
    h1 = x.astype(BF16)
    r1 = x - h1.astype(F32)
    h2 = r1.astype(BF16)
    h3 = (r1 - h2.astype(F32)).astype(BF16)
    return h1, h2, h3


def _split2(x):
    h1 = x.astype(BF16)
    h2 = (x - h1.astype(F32)).astype(BF16)
    return h1, h2


def _group_sum(x, bd):
    w = bd.shape[0]
    hi, lo = _split2(x)
    outs = []
    for c in range(x.shape[1] // w):
        sl = slice(c * w, (c + 1) * w)
        outs.append(_dot(hi[:, sl], bd) + _dot(lo[:, sl], bd))
    return outs[0] if len(outs) == 1 else jnp.concatenate(outs, axis=1)


def _group_sums(xs, bd):
    rows = xs[0].shape[0]
    parts = []
    for x in xs:
        parts.extend(_split2(x))
    out = _dot(jnp.concatenate(parts, axis=0), bd)
    return [out[2 * k * rows:(2 * k + 1) * rows] + out[(2 * k + 1) * rows:(2 * k + 2) * rows]
            for k in range(len(xs))]


def _cumsum_rows(tri, x):
    h1, h2, h3 = _split3(x)
    return _dot(tri, h1) + _dot(tri, h2) + _dot(tri, h3)


def _prefix_sum_rows(x):
    row = lax.broadcasted_iota(jnp.int32, x.shape, 0)
    shift = 1
    while shift < x.shape[0]:
        x = x + jnp.where(row >= shift, pltpu.roll(x, shift, 0), 0.0)
        shift *= 2
    return x


def _sigmoid(x):
    return 1.0 / (1.0 + jnp.exp(-x))


def _softplus(x):
    return jnp.maximum(x, 0.0) + jnp.log1p(jnp.exp(-jnp.abs(x)))


def _rms(x, g):
    ms = jnp.mean(x * x, axis=-1, keepdims=True)
    return x * lax.rsqrt(ms + NORM_EPS) * g


def _params(sem):
    return pltpu.CompilerParams(dimension_semantics=sem, vmem_limit_bytes=VMEM_LIMIT)


def _in_proj_kernel(x_ref, g_ref, w_ref, o_ref, xn_ref):
    j = pl.program_id(1)
    tn = o_ref.shape[1]

    @pl.when(j == 0)
    def _():
        xn_ref[...] = _rms(x_ref[...], g_ref[...]).astype(BF16)

    o_ref[...] = _dot(xn_ref[...], w_ref[:, pl.ds(pl.multiple_of(j * tn, tn), tn)])


def _in_proj(h, g, w, *, tm, tn):
    t, d = h.shape
    n = w.shape[1]
    return pl.pallas_call(
        _in_proj_kernel,
        out_shape=jax.ShapeDtypeStruct((t, n), F32),
        grid=(t // tm, n // tn),
        in_specs=[
            pl.BlockSpec((tm, d), lambda i, j: (i, 0)),
            pl.BlockSpec((1, d), lambda i, j: (0, 0)),
            pl.BlockSpec((d, n), lambda i, j: (0, 0), pipeline_mode=pl.Buffered(1)),
        ],
        out_specs=pl.BlockSpec((tm, tn), lambda i, j: (i, j)),
        scratch_shapes=[pltpu.VMEM((tm, d), BF16)],
        compiler_params=_params(("parallel", "arbitrary")),
        name="in_proj",
    )(h, g, w)


def _rwkv_prep_kernel(zr_ref, zl_ref, pr_ref, pl_ref, mur_ref, mul_ref, w0_ref, w2_ref, a0_ref,
                      a2_ref, g2_ref, kkw_ref, kaw_ref, bd_ref,
                      r_out, ld_out, kp_out, v_out, kk_out, b_out, g_out, *, blocks_per_seq):
    first = (pl.program_id(0) % blocks_per_seq) == 0

    def shift(z, prev8, mu):
        last = jnp.where(first, 0.0, prev8[7:8, :])
        rolled = pltpu.roll(z, 1, 0)
        row = lax.broadcasted_iota(jnp.int32, z.shape, 0)
        prev = jnp.where(row == 0, last, rolled)
        return z + (prev - z) * mu

    zs = shift(zr_ref[...], pr_ref[...], mur_ref[...])
    zl = shift(zl_ref[...], pl_ref[...], mul_ref[...])
    r = zs[:, :D_RWKV]
    k = zs[:, D_RWKV:2 * D_RWKV]
    v = zs[:, 2 * D_RWKV:]

    lw = w0_ref[...] + _dot(jnp.tanh(zl).astype(BF16), w2_ref[...])
    w_log2 = ((jnp.minimum(lw, 0.0) - 0.5) * LOG2E
              - jnp.log2(1.0 + jnp.exp2(jnp.abs(lw) * -LOG2E)))
    ld = -jnp.exp2(w_log2)
    a = _sigmoid(a0_ref[...] + _dot(zl.astype(BF16), a2_ref[...]))
    g = _dot(_sigmoid(zl).astype(BF16), g2_ref[...])

    kkr = k * kkw_ref[...]
    ss = _group_sum(kkr * kkr, bd_ref[...])
    kk = kkr * lax.rsqrt(jnp.maximum(ss, 1e-24))
    kp = k * (1.0 + (a - 1.0) * kaw_ref[...])

    r_out[...] = r.astype(r_out.dtype)
    ld_out[...] = ld
    kp_out[...] = kp.astype(kp_out.dtype)
    v_out[...] = v.astype(v_out.dtype)
    kk_out[...] = kk.astype(kk_out.dtype)
    b_out[...] = (kk * a).astype(b_out.dtype)
    g_out[...] = g.astype(g_out.dtype)


def _rwkv_prep(z, seq, mu_r, mu_l, w0, w2p, a0, a2p, g2p, k_k, k_a, bd, *, tm):
    t = z.shape[0]
    c_lora = (3 * D_RWKV + 3 * D_FOX) // N_LORA_PAD
    rows8 = tm // 8
    row = lambda i: (0, 0)
    prev = lambda i: (jnp.maximum(i * rows8 - 1, 0), 0)
    out = [jax.ShapeDtypeStruct((t, D_RWKV), F32 if name == "ld" else BF16)
           for name in ("r", "ld", "kp", "v", "kk", "b", "g")]
    return pl.pallas_call(
        functools.partial(_rwkv_prep_kernel, blocks_per_seq=seq // tm),
        out_shape=out,
        grid=(t // tm,),
        in_specs=[
            pl.BlockSpec((tm, 3 * D_RWKV), lambda i: (i, 0)),
            pl.BlockSpec((tm, N_LORA_PAD), lambda i: (i, c_lora)),
            pl.BlockSpec((8, 3 * D_RWKV), prev),
            pl.BlockSpec((8, N_LORA_PAD), lambda i: (jnp.maximum(i * rows8 - 1, 0), c_lora)),
            pl.BlockSpec((1, 3 * D_RWKV), row),
            pl.BlockSpec((1, N_LORA_PAD), row),
            pl.BlockSpec((1, D_RWKV), row),
            pl.BlockSpec((N_LORA_PAD, D_RWKV), row),
            pl.BlockSpec((1, D_RWKV), row),
            pl.BlockSpec((N_LORA_PAD, D_RWKV), row),
            pl.BlockSpec((N_LORA_PAD, D_RWKV), row),
            pl.BlockSpec((1, D_RWKV), row),
            pl.BlockSpec((1, D_RWKV), row),
            pl.BlockSpec(bd.shape, row),
        ],
        out_specs=[pl.BlockSpec((tm, D_RWKV), lambda i: (i, 0))] * 7,
        compiler_params=_params(("parallel",)),
        name="rwkv_prep",
    )(z, z, z, z, mu_r, mu_l, w0, w2p, a0, a2p, g2p, k_k, k_a, bd)


def _stack_heads(q, masks):
    zero = jnp.zeros_like(q)
    return jnp.concatenate([jnp.where(m, q, zero) for m in masks], axis=0).astype(BF16)


def _diag_blocks(full, masks):
    out = full[:HEAD_DIM]
    for g in range(1, len(masks)):
        out = jnp.where(masks[g], full[g * HEAD_DIM:(g + 1) * HEAD_DIM], out)
    return out


def _scan_kernel(r_ref, ld_ref, kp_ref, v_ref, kk_ref, b_ref, g_ref, gnw_ref, gnb_ref, rk_ref,
                 bd_ref, y_ref, s_ref, *, nchunk):
    L = CHUNK
    W = bd_ref.shape[0]
    ngroup = s_ref.shape[1] // W
    lanes = [slice(gp * W, (gp + 1) * W) for gp in range(ngroup)]
    unit_group = [gp for _ in range(nchunk) for gp in range(ngroup)]
    chunks = range(nchunk * ngroup)
    sl = [(slice(c * L, (c + 1) * L), lanes[gp]) for c in range(nchunk) for gp in range(ngroup)]

    @pl.when(pl.program_id(2) == 0)
    def _():
        s_ref[...] = jnp.zeros_like(s_ref)

    lane = lax.broadcasted_iota(jnp.int32, (1, W), 1)
    masks = [lane // HEAD_DIM == g for g in range(W // HEAD_DIM)]
    row = lax.broadcasted_iota(jnp.int32, (L, W), 0)
    col = lax.broadcasted_iota(jnp.int32, (L, W), 1) % HEAD_DIM
    strict = row > col
    incl = row >= col
    eye = (row == col).astype(F32)
    bd = bd_ref[...]
    zero = jnp.zeros((L, W), F32)

    def stack(q):
        return _stack_heads(q, masks)

    def hmm(p, qst):
        return _dot(p.astype(BF16), qst)

    ld = [ld_ref[s] for s in sl]
    cum = [_prefix_sum_rows(x) for x in ld]
    w_last = [jnp.exp(x[L - 1:L, :]) for x in cum]
    w_inv = [jnp.exp(-x) for x in cum]
    r = [r_ref[s].astype(F32) for s in sl]
    kp = [kp_ref[s].astype(F32) for s in sl]
    v = [v_ref[s].astype(F32) for s in sl]
    abar = [-kk_ref[sl[c]].astype(F32) * jnp.exp(cum[c] - ld[c]) for c in chunks]
    rbar = [r[c] * jnp.exp(cum[c]) for c in chunks]
    btil = [b_ref[sl[c]].astype(F32) * w_inv[c] for c in chunks]
    ktil = [kp[c] * w_inv[c] for c in chunks]
    bk = [jnp.concatenate([btil[c] * w_last[c], ktil[c] * w_last[c]], axis=0).astype(BF16)
          for c in chunks]
    x = [jnp.concatenate([abar[c], rbar[c]], axis=0).astype(BF16) for c in chunks]

    a_all = [_dot_nt(x[c], jnp.concatenate([stack(btil[c]), stack(ktil[c])], axis=0))
             for c in chunks]
    n = [jnp.where(strict, a[:L, :W], zero) for a in a_all]
    a_ak = [jnp.where(strict, a[:L, W:], zero) for a in a_all]
    a_rb = [jnp.where(incl, a[L:, :W], zero) for a in a_all]
    a_rk = [jnp.where(incl, a[L:, W:], zero) for a in a_all]

    vst = [stack(x) for x in v]
    av = [hmm(jnp.concatenate([a_ak[c], a_rk[c]], axis=0), vst[c]) for c in chunks]

    tinv = [eye + x for x in n]
    nk = [hmm(x, stack(x)) for x in n]
    for _ in range(4):
        both = [hmm(jnp.concatenate([tinv[c], nk[c]], axis=0), stack(nk[c])) for c in chunks]
        tinv = [tinv[c] + both[c][:L] for c in chunks]
        nk = [x[L:] for x in both]
    tinv = [tinv[c] + hmm(tinv[c], stack(nk[c])) for c in chunks]

    s = [s_ref[:, ln] for ln in lanes]
    ys = [None] * len(chunks)
    for first in range(0, len(chunks), ngroup):
        ids = range(first, first + ngroup)
        xs = [_dot_nt(x[c], stack(s[unit_group[c]])) for c in ids]
        u = [hmm(tinv[c], stack(xs[c - first][:L] + av[c][:L])) for c in ids]
        for c in ids:
            ys[c] = xs[c - first][L:] + hmm(a_rb[c], stack(u[c - first])) + av[c][L:]
        upd = [_diag_blocks(_dot_tn(jnp.concatenate([u[c - first], v[c]], axis=0).astype(BF16),
                                    bk[c]), masks) for c in ids]
        for c in ids:
            s[unit_group[c]] = s[unit_group[c]] * w_last[c] + upd[c - first]
    for gp, ln in enumerate(lanes):
        s_ref[:, ln] = s[gp]

    rk = [rk_ref[:, ln] for ln in lanes]
    sums = _group_sums(ys + [r[c] * kp[c] * rk[unit_group[c]] for c in chunks], bd)
    ds = [ys[c] - sums[c] * (1.0 / HEAD_DIM) for c in chunks]
    var = _group_sums([d * d for d in ds], bd)
    for c in chunks:
        ln = lanes[unit_group[c]]
        yn = ds[c] * lax.rsqrt(var[c] * (1.0 / HEAD_DIM) + GN_EPS) * gnw_ref[:, ln] + gnb_ref[:, ln]
        bonus = sums[len(chunks) + c] * v[c]
        y_ref[sl[c]] = ((yn + bonus) * g_ref[sl[c]].astype(F32)).astype(y_ref.dtype)


def _rwkv_scan(feats, g, gn_w, gn_b, r_k, bd, batch, seq, *, tc, ngroup):
    gw = ngroup * bd.shape[0]
    tok = pl.BlockSpec((None, tc, gw), lambda b, p, c: (b, c, p))
    par = pl.BlockSpec((1, gw), lambda b, p, c: (0, p))
    feats = [f.reshape(batch, seq, D_RWKV) for f in feats]
    return pl.pallas_call(
        functools.partial(_scan_kernel, nchunk=tc // CHUNK),
        out_shape=jax.ShapeDtypeStruct((batch, seq, D_RWKV), BF16),
        grid=(batch, D_RWKV // gw, seq // tc),
        in_specs=[tok] * 7 + [par] * 3 + [pl.BlockSpec(bd.shape, lambda b, p, c: (0, 0))],
        out_specs=tok,
        scratch_shapes=[pltpu.VMEM((HEAD_DIM, gw), F32)],
        compiler_params=_params(("parallel", "parallel", "arbitrary")),
        name="rwkv_scan",
    )(*feats, g.reshape(batch, seq, D_RWKV), gn_w, gn_b, r_k, bd)


def _fox_prep_kernel(q_ref, k_ref, v_ref, ff_ref, gq_ref, gk_ref, bf_ref, bd_ref, sel_ref, one_ref,
                     qt_ref, ka_ref, vt_ref, ct_ref, carry_ref, *, blocks_per_seq):
    @pl.when((pl.program_id(0) % blocks_per_seq) == 0)
    def _():
        carry_ref[...] = jnp.zeros_like(carry_ref)

    bd = bd_ref[...]
    q = q_ref[...]
    k = k_ref[...]
    msq = _group_sum(q * q, bd) * (1.0 / HEAD_DIM)
    msk = _group_sum(k * k, bd) * (1.0 / HEAD_DIM)
    qn = q * lax.rsqrt(msq + NORM_EPS) * gq_ref[...] * (HEAD_DIM ** -0.5 * LOG2E)
    kn = k * lax.rsqrt(msk + NORM_EPS) * gk_ref[...]
    qt_ref[...] = qn.T.astype(BF16)
    vt_ref[...] = v_ref[...].T.astype(BF16)

    tm = q.shape[0]
    log_f = -_softplus(-(ff_ref[...] + bf_ref[...])) * LOG2E
    tri = (lax.broadcasted_iota(jnp.int32, (tm, tm), 0)
           >= lax.broadcasted_iota(jnp.int32, (tm, tm), 1)).astype(BF16)
    c_local = _cumsum_rows(tri, log_f)
    c = c_local + carry_ref[...]
    carry_ref[...] = c[tm - 1:tm, :]
    ct_ref[...] = c.T[:ct_ref.shape[0], :]

    n1, n2, n3 = _split3(-c_local)
    aug = _dot(n1, sel_ref[0]) + _dot(n2, sel_ref[1]) + _dot(n3, sel_ref[2]) + one_ref[...]
    for p in range(D_FOX // LANES):
        src = slice(p * LANES, (p + 1) * LANES)
        ka_ref[:, 2 * p * LANES:(2 * p + 1) * LANES] = kn[:, src].astype(BF16)
        ka_ref[:, (2 * p + 1) * LANES:(2 * p + 2) * LANES] = aug[:, src].astype(BF16)


def _fox_prep(z, seq, gq, gk, bf, bd, sel, one):
    t = z.shape[0]
    tm = TK
    nh = D_FOX // HEAD_DIM
    c_ff = (3 * D_RWKV + 3 * D_FOX + N_LORA_PAD) // N_FF_PAD
    row = lambda i: (0, 0)
    tr = jax.ShapeDtypeStruct((D_FOX, t), BF16)
    return pl.pallas_call(
        functools.partial(_fox_prep_kernel, blocks_per_seq=seq // tm),
        out_shape=[tr, jax.ShapeDtypeStruct((t, 2 * D_FOX), BF16), tr,
                   jax.ShapeDtypeStruct((nh, t), F32)],
        grid=(t // tm,),
        in_specs=[
            pl.BlockSpec((tm, D_FOX), lambda i: (i, 3)),
            pl.BlockSpec((tm, D_FOX), lambda i: (i, 4)),
            pl.BlockSpec((tm, D_FOX), lambda i: (i, 5)),
            pl.BlockSpec((tm, N_FF_PAD), lambda i: (i, c_ff)),
            pl.BlockSpec((1, D_FOX), row),
            pl.BlockSpec((1, D_FOX), row),
            pl.BlockSpec((1, N_FF_PAD), row),
            pl.BlockSpec(bd.shape, row),
            pl.BlockSpec(sel.shape, lambda i: (0, 0, 0)),
            pl.BlockSpec((1, D_FOX), row),
        ],
        out_specs=[pl.BlockSpec((D_FOX, tm), lambda i: (0, i)),
                   pl.BlockSpec((tm, 2 * D_FOX), lambda i: (i, 0)),
                   pl.BlockSpec((D_FOX, tm), lambda i: (0, i)),
                   pl.BlockSpec((nh, tm), lambda i: (0, i))],
        scratch_shapes=[pltpu.VMEM((1, N_FF_PAD), F32)],
        compiler_params=_params(("arbitrary",)),
        name="fox_prep",
    )(z, z, z, z, gq, gk, bf, bd, sel, one)


def _attn_kernel(thr_ref, cref_ref, qt_ref, ka_ref, vt_ref, ct_ref, o_ref, acc_ref, pt_ref,
                 stat_ref, *, nblk):
    def qblock(i, nlive):
        return _attn_qblock(i, nlive, thr_ref, cref_ref, qt_ref, ka_ref, vt_ref, ct_ref, o_ref,
                            acc_ref, pt_ref, stat_ref, nblk=nblk)

    lax.fori_loop(0, nblk, qblock, jnp.int32(0))


def _live_blocks(i, thr, cref_ref, base, nblk):
    ci = [cref_ref[b + i] for b in base]
    lo = [jnp.int32(0)] * len(base)
    hi = [i] * len(base)
    for _ in range(nblk.bit_length()):
        for e, b in enumerate(base):
            mid = (lo[e] + hi[e]) // 2
            live = ci[e] - cref_ref[b + i - mid] >= thr
            is_open = lo[e] < hi[e]
            lo[e], hi[e] = (jnp.where(jnp.logical_and(is_open, live), mid + 1, lo[e]),
                            jnp.where(jnp.logical_and(is_open, jnp.logical_not(live)), mid, hi[e]))
    return functools.reduce(jnp.maximum, lo)


def _attn_qblock(i, nlive, thr_ref, cref_ref, qt_ref, ka_ref, vt_ref, ct_ref, o_ref, acc_ref,
                 pt_ref, stat_ref, *, nblk):
    b = pl.program_id(0)
    p = pl.program_id(1)
    nh = ct_ref.shape[0]
    tq = TK
    qoff = pl.multiple_of(i * tq, tq)

    row_d = lax.broadcasted_iota(jnp.int32, (LANES, 1), 0)
    qt = qt_ref[:, pl.ds(qoff, tq)]
    zq = jnp.zeros_like(qt)
    qte = (jnp.where(row_d < HEAD_DIM, qt, zq), jnp.where(row_d < HEAD_DIM, zq, qt))
    rs = lax.broadcasted_iota(jnp.int32, (16, tq), 0)
    zpad = jnp.zeros((LANES - 16, tq), BF16)
    base = tuple((b * nh + 2 * p + e) * (nblk + 1) for e in range(2))
    ci = tuple(cref_ref[base[e] + i] for e in range(2))

    def rhs_operand(e):
        d = ct_ref[pl.ds(2 * p + e, 1), pl.ds(qoff, tq)] - ci[e]
        d1 = d.astype(BF16).astype(F32)
        r1 = d - d1
        d2 = r1.astype(BF16).astype(F32)
        d3 = r1 - d2
        ones3 = ((rs >= AUG * e) & (rs < AUG * e + 3)).astype(F32)
        strip = jnp.where(rs == AUG * e + 3, d1,
                          jnp.where(rs == AUG * e + 4, d2,
                                    jnp.where(rs == AUG * e + 5, d3, ones3)))
        return jnp.concatenate([qte[e], strip.astype(BF16), zpad], axis=0)

    rhs = tuple(rhs_operand(e) for e in range(2))

    def logits(off, e):
        return _dot(ka_ref[pl.ds(off, TK), :], rhs[e])

    def block_shift(j, e):
        return ci[e] - cref_ref[base[e] + j]

    def col_reduce(x, op):
        return op(op(x.reshape(8, x.shape[0] // 8, x.shape[1]), axis=0), axis=0, keepdims=True)

    def softmax_step(st, e, slot, shift):
        m = stat_ref[e, 0:1, :]
        m_new = jnp.maximum(m, col_reduce(st, jnp.max) + shift)
        alpha = jnp.exp2(m - m_new)
        pt = jnp.exp2(st - (m_new - shift))
        pt_ref[slot, e] = pt.astype(BF16)
        stat_ref[e, 0:1, :] = m_new
        stat_ref[e, 1:2, :] = alpha * stat_ref[e, 1:2, :] + col_reduce(pt, jnp.sum)
        stat_ref[e, 2 + slot:3 + slot, :] = alpha

    def pv_step(off, e, slot):
        vte = vt_ref[pl.ds(HEAD_DIM * e, HEAD_DIM), pl.ds(off, TK)]
        acc_ref[e] = (stat_ref[e, 2 + slot:3 + slot, :] * acc_ref[e]
                      + _dot(vte, pt_ref[slot, e]))

    acc_ref[...] = jnp.zeros_like(acc_ref)
    for e in range(2):
        stat_ref[e, 0:1, :] = jnp.full((1, tq), -jnp.inf, F32)
        stat_ref[e, 1:2, :] = jnp.zeros((1, tq), F32)

    thr = thr_ref[0]

    nlive_next = _live_blocks(i + 1, thr, cref_ref, base, nblk)

    def step(first, nb, diagonal):
        offs = [pl.multiple_of((first - t) * TK, TK) for t in range(nb)]
        sts = [None] * nb
        for t in range(min(LOOKAHEAD, nb)):
            sts[t] = tuple(logits(offs[t], e) for e in range(2))
        if diagonal:
            tt = lax.broadcasted_iota(jnp.int32, (TK, tq), 0)
            rr = lax.broadcasted_iota(jnp.int32, (TK, tq), 1)
            sts[0] = tuple(jnp.where(tt <= rr, st, -jnp.inf) for st in sts[0])
        else:
            for e in range(2):
                pv_step(pl.multiple_of(offs[0] + TK, TK), e, 0)
        for t in range(nb):
            slot = (nb - 1 - t) % 2
            if t + LOOKAHEAD < nb:
                sts[t + LOOKAHEAD] = tuple(logits(offs[t + LOOKAHEAD], e) for e in range(2))
            for e in range(2):
                softmax_step(sts[t][e], e, slot, block_shift(first - t, e))
            if t < nb - 1:
                for e in range(2):
                    pv_step(offs[t], e, slot)

    rem = (nlive + 1) % UNROLL
    nb = 1
    while nb < UNROLL:
        below = rem & (nb - 1)
        pl.when(jnp.logical_and((rem & nb) != 0, below == 0))(
            functools.partial(step, i, nb, True))
        if nb > 1:
            pl.when(jnp.logical_and((rem & nb) != 0, below != 0))(
                functools.partial(step, i - below, nb, False))
        nb *= 2
    pl.when(rem == 0)(functools.partial(step, i, UNROLL, True))
    head = jnp.where(rem == 0, UNROLL, rem)

    def body(k, carry):
        step(i - head - UNROLL * k, UNROLL, False)
        return carry

    lax.fori_loop(0, (nlive + 1 - head) // UNROLL, body, 0)
    last = pl.multiple_of((i - nlive) * TK, TK)
    for e in range(2):
        pv_step(last, e, 0)
    ot = jnp.concatenate([acc_ref[e] / stat_ref[e, 1:2, :] for e in range(2)], axis=0)
    o_ref[pl.ds(qoff, tq), :] = ot.T.astype(o_ref.dtype)
    return nlive_next


def _fox_attention(thr, cref, qt, ka, vt, ct, batch, seq):
    tq = TK
    npair = D_FOX // LANES
    nh = D_FOX // HEAD_DIM
    smem = pl.BlockSpec(memory_space=pltpu.SMEM)
    return pl.pallas_call(
        functools.partial(_attn_kernel, nblk=seq // TK),
        out_shape=jax.ShapeDtypeStruct((batch, seq, D_FOX), BF16),
        grid=(batch, npair),
        in_specs=[
            smem, smem,
            pl.BlockSpec((LANES, seq), lambda b, p: (p, b)),
            pl.BlockSpec((seq, 2 * LANES), lambda b, p: (b, p)),
            pl.BlockSpec((LANES, seq), lambda b, p: (p, b)),
            pl.BlockSpec((nh, seq), lambda b, p: (0, b)),
        ],
        out_specs=pl.BlockSpec((None, seq, LANES), lambda b, p: (b, 0, p)),
        scratch_shapes=[pltpu.VMEM((2, HEAD_DIM, tq), F32), pltpu.VMEM((2, 2, TK, tq), BF16),
                        pltpu.VMEM((2, 8, tq), F32)],
        compiler_params=_params(("parallel", "parallel")),
        name="fox_attn",
    )(thr, cref, qt, ka, vt, ct)


def _out_proj_kernel(h_ref, y1_ref, y2_ref, w1_ref, w2_ref, o_ref):
    o_ref[...] = h_ref[...] + _dot(y1_ref[...], w1_ref[...]) + _dot(y2_ref[...], w2_ref[...])


def _resident(shape):
    return pl.BlockSpec(shape, lambda i: (0,) * len(shape), pipeline_mode=pl.Buffered(1))


def _out_proj(h, y1, y2, w1, w2, *, tm):
    t, d = h.shape
    return pl.pallas_call(
        _out_proj_kernel,
        out_shape=jax.ShapeDtypeStruct((t, d), F32),
        grid=(t // tm,),
        in_specs=[
            pl.BlockSpec((tm, d), lambda i: (i, 0)),
            pl.BlockSpec((tm, y1.shape[1]), lambda i: (i, 0)),
            pl.BlockSpec((tm, y2.shape[1]), lambda i: (i, 0)),
            _resident(w1.shape),
            _resident(w2.shape),
        ],
        out_specs=pl.BlockSpec((tm, d), lambda i: (i, 0)),
        compiler_params=_params(("parallel",)),
        name="out_proj",
    )(h, y1, y2, w1, w2)


def _mlp_kernel(x_ref, g_ref, w1_ref, w2_ref, o_ref, xn_ref):
    @pl.when(pl.program_id(1) == 0)
    def _():
        x = x_ref[...]
        xn_ref[...] = _rms(x, g_ref[...]).astype(BF16)
        o_ref[...] = x

    hid = jnp.maximum(_dot(xn_ref[...], w1_ref[...]), 0.0)
    o_ref[...] += _dot((hid * hid).astype(BF16), w2_ref[...])


def _mlp(h, g, w1, w2, layer, *, tm, tf):
    t, d = h.shape
    f = w1.shape[2]
    return pl.pallas_call(
        _mlp_kernel,
        out_shape=jax.ShapeDtypeStruct((t, d), F32),
        grid=(t // tm, f // tf),
        in_specs=[
            pl.BlockSpec((tm, d), lambda i, j: (i, 0)),
            pl.BlockSpec((1, d), lambda i, j: (0, 0)),
            pl.BlockSpec((None, d, tf), lambda i, j: (layer, 0, j)),
            pl.BlockSpec((None, tf, d), lambda i, j: (layer, j, 0)),
        ],
        out_specs=pl.BlockSpec((tm, d), lambda i, j: (i, 0)),
        scratch_shapes=[pltpu.VMEM((tm, d), BF16)],
        compiler_params=_params(("parallel", "arbitrary")),
        name="mlp",
    )(h, g, w1, w2)


def _ple_kernel(x_ref, g_ref, p_ref, wg_ref, wp_ref, gf_ref, o_ref, *, final):
    x = x_ref[...]
    gate = _sigmoid(_dot(_rms(x, g_ref[...]).astype(BF16), wg_ref[...]))
    h = x + gate * _dot(p_ref[...].astype(BF16), wp_ref[...])
    o_ref[...] = _rms(h, gf_ref[...]) if final else h


def _ple(h, g, p, layer, wg, wp, g_final, *, tm, final):
    t, d = h.shape
    dp = p.shape[2]
    return pl.pallas_call(
        functools.partial(_ple_kernel, final=final),
        out_shape=jax.ShapeDtypeStruct((t, d), F32),
        grid=(t // tm,),
        in_specs=[
            pl.BlockSpec((tm, d), lambda i: (i, 0)),
            _resident((1, d)),
            pl.BlockSpec((None, tm, dp), lambda i: (layer, i, 0)),
            _resident(wg.shape),
            _resident(wp.shape),
            _resident((1, d)),
        ],
        out_specs=pl.BlockSpec((tm, d), lambda i: (i, 0)),
        compiler_params=_params(("parallel",)),
        name="ple",
    )(h, g, p, wg, wp, g_final)


def _block_diag_ones(n):
    idx = np.arange(n) // HEAD_DIM
    return jnp.asarray(idx[:, None] == idx[None, :], BF16)


def _pad_cols(w, n):
    return jnp.pad(w, ((0, 0), (0, n - w.shape[1])))


def _take_heads(w, order, axis):
    shape = w.shape
    split = shape[:axis] + (order.shape[0], shape[axis] // order.shape[0]) + shape[axis + 1:]
    return jnp.take(w.reshape(split), order, axis=axis).reshape(shape)


def _pack_w_in(w, order):
    n_shift = 3 * D_RWKV + N_LORA
    rkv = w[:, :3 * D_RWKV]
    lora = _pad_cols(w[:, 3 * D_RWKV:n_shift], N_LORA_PAD)
    fox = [_take_heads(w[:, n_shift + k * D_FOX:n_shift + (k + 1) * D_FOX], order, 1) for k in range(3)]
    ff = _pad_cols(jnp.take(w[:, n_shift + 3 * D_FOX:], order, axis=1), N_FF_PAD)
    return jnp.concatenate([rkv] + fox + [lora, ff], axis=1).astype(BF16)


def _bias_lane_tables():
    nh = D_FOX // HEAD_DIM
    h = np.arange(nh)
    lane = (h // 2) * LANES + (h % 2) * AUG
    sel = np.zeros((3, N_FF_PAD, D_FOX), np.float32)
    one = np.zeros((1, D_FOX), np.float32)
    for k in range(3):
        sel[k, h, lane + k] = 1.0
        one[0, lane + 3 + k] = 1.0
    return jnp.asarray(sel, BF16), jnp.asarray(one)


def _skip_threshold(gq, gk):
    smax = 1.02 * 8.0 * jnp.max(jnp.abs(gq)) * jnp.max(jnp.abs(gk)) * LOG2E
    return (-(EXP2_UNDERFLOW + 2.0 * smax)).reshape(1).astype(F32)


def _block_starts(ct, batch, seq):
    nh = ct.shape[0]
    ends = ct.reshape(nh, batch, seq // TK, TK)[..., TK - 1]
    starts = jnp.pad(ends, ((0, 0), (0, 0), (1, 0)))
    return starts.transpose(1, 0, 2).reshape(-1)


def _lora_rows(w, start):
    return jnp.pad(w, ((start, N_LORA_PAD - start - w.shape[0]), (0, 0))).astype(BF16)


def kernel(x, p, norm_mix, w_in, mu_shift, w0, w2, a0, a2, g2, k_k, k_a, r_k, gn_w, gn_b, b_f, q_norm, k_norm, w_out, norm_mlp, w_ff1, w_ff2, norm_ple, w_ple_gate, w_ple_proj, norm_final):
    batch, seq, d = x.shape
    depth = w_in.shape[0]
    t = batch * seq
    nh_fox = D_FOX // HEAD_DIM
    tm = min(512, seq)
    sel, one = _bias_lane_tables()
    bd256 = _block_diag_ones(256)
    bd128 = _block_diag_ones(LANES)
    row = lambda a: a.reshape(1, -1)

    w1_all = w_ff1.astype(BF16)
    w2_all = w_ff2.astype(BF16)
    h = x.reshape(t, d)
    for i in range(depth):
        order = jnp.argsort(b_f[i])
        z = _in_proj(h, row(norm_mix[i]), _pack_w_in(w_in[i], order), tm=min(1024, seq), tn=512)

        mu = mu_shift[i]
        feats = _rwkv_prep(
            z, seq, row(mu[:3 * D_RWKV]), row(jnp.pad(mu[3 * D_RWKV:], (0, N_LORA_PAD - N_LORA))),
            row(w0[i]), _lora_rows(w2[i], 0), row(a0[i]), _lora_rows(a2[i], 64),
            _lora_rows(g2[i], 128), row(k_k[i]), row(k_a[i]), bd256, tm=min(256, seq))
        y_rw = _rwkv_scan(feats[:6], feats[6], row(gn_w[i]), row(gn_b[i]), row(r_k[i]), bd256,
                          batch, seq, tc=min(256, seq), ngroup=4)

        qt, ka, vt, ct = _fox_prep(
            z, seq, row(jnp.tile(q_norm[i], nh_fox)), row(jnp.tile(k_norm[i], nh_fox)),
            row(jnp.pad(jnp.take(b_f[i], order), (0, N_FF_PAD - nh_fox))), bd256, sel, one)
        y_fox = _fox_attention(_skip_threshold(q_norm[i], k_norm[i]), _block_starts(ct, batch, seq),
                               qt, ka, vt, ct, batch, seq)

        wo = w_out[i].astype(BF16)
        h = _out_proj(h, y_rw.reshape(t, D_RWKV), y_fox.reshape(t, D_FOX), wo[:D_RWKV],
                      _take_heads(wo[D_RWKV:], order, 0), tm=tm)
        h = _mlp(h, row(norm_mlp[i]), w1_all, w2_all, i, tm=min(1024, seq), tf=512)
        h = _ple(h, row(norm_ple[i]), p.reshape(depth, t, -1), i, w_ple_gate[i].astype(BF16),
                 w_ple_proj[i].astype(BF16), row(norm_final), tm=tm, final=i == depth - 1)
    return h.reshape(batch, seq, d)
```
